```python
import jax, jax.numpy as jnp
from jax import lax
import numpy as np

D_MODEL = 1024
BATCH = 32
SEQ = 2048
DEPTH = 2
DEC_BATCH = 128
DEC_SEQ = 1
PAST_LEN = 16384
PAGE_SIZE = 128

HEAD_DIM = 64
BR_WIDTH = D_MODEL // 4
N_BRANCH = 4
ROPE_THETA = 500000.0
ROT_DIM = HEAD_DIM // 4
NORM_EPS = 1e-6
NEG_INF = -1e30
D_FF = 4 * D_MODEL
CHUNK = 128
A_DIM = HEAD_DIM
A_GROUPS = BR_WIDTH // A_DIM
FOX_HEADS = BR_WIDTH // HEAD_DIM
FOX_KV_HEADS = 2
FOX_GROUP = FOX_HEADS // FOX_KV_HEADS
FOX_BIAS_CENTER = 2.0
MLA_HEADS = 4
MLA_NOPE = 64
MLA_ROPE = 32
MLA_V = BR_WIDTH // MLA_HEADS
MLA_Q_LORA = 256
MLA_KV_LORA = 128
MOBA_HEADS = BR_WIDTH // HEAD_DIM
MOBA_KV_HEADS = 2
MOBA_GROUP = MOBA_HEADS // MOBA_KV_HEADS
MOBA_BLOCK = 256
MOBA_TOPK = 3
MOBA_Q_BLOCK = 16
ATTN_Q_BLOCK = 128

IN_SPLITS = (BR_WIDTH, BR_WIDTH,
             FOX_HEADS * HEAD_DIM, FOX_KV_HEADS * HEAD_DIM, FOX_KV_HEADS * HEAD_DIM, FOX_HEADS,
             MLA_Q_LORA, MLA_KV_LORA, MLA_ROPE,
             MOBA_HEADS * HEAD_DIM, MOBA_KV_HEADS * HEAD_DIM, MOBA_KV_HEADS * HEAD_DIM)
D_IN = sum(IN_SPLITS)

kernel_name = 'hybrid_gated_chunkmlp_fox_mla_moba_step'


def rmsnorm(x, g):
    xf = x.astype(jnp.float32)
    y = xf * lax.rsqrt(jnp.mean(xf * xf, axis=-1, keepdims=True) + NORM_EPS)
    return (y * g.astype(jnp.float32)).astype(x.dtype)


def rope(x, pos, rot_dim):
    half = rot_dim // 2
    inv_freq = ROPE_THETA ** (-jnp.arange(half, dtype=jnp.float32) / half)
    ang = pos.astype(jnp.float32)[:, None] * inv_freq[None, :]
    cos = jnp.cos(ang)[:, None, :]
    sin = jnp.sin(ang)[:, None, :]
    xr = x[..., :rot_dim].astype(jnp.float32)
    x1, x2 = xr[..., :half], xr[..., half:]
    rot = jnp.concatenate([x1 * cos - x2 * sin, x2 * cos + x1 * sin], axis=-1).astype(x.dtype)
    return jnp.concatenate([rot, x[..., rot_dim:]], axis=-1)


def split_in(z):
    cuts = np.cumsum(IN_SPLITS)[:-1].tolist()
    return jnp.split(z, cuts, axis=-1)


def causal_mask(n):
    return jnp.tril(jnp.ones((n, n), dtype=bool))


def softmax_parts(parts):
    p = jax.nn.softmax(jnp.concatenate(parts, axis=-1), axis=-1)
    cuts = np.cumsum([s.shape[-1] for s in parts])[:-1].tolist()
    return jnp.split(p, cuts, axis=-1)


def spatial_gate(u, vn, w_s, b_s):
    B, S, G, Dg = u.shape
    L = min(S, CHUNK)
    n_chunks = S // L
    ws = jnp.where(causal_mask(L), w_s[:, :L, :L], 0.0)
    vc = vn.reshape(B, n_chunks, L, G, Dg)
    s = jnp.einsum('gts,bcsgd->bctgd', ws, vc) + jnp.transpose(b_s[:, :L])[None, None, :, :, None]
    return u * s.reshape(B, S, G, Dg)


def blocked_causal_attention(q, k, v, c=None):
    B, S, KH, G, Dk = q.shape
    scale = Dk ** -0.5
    Q = ATTN_Q_BLOCK
    kpos = jnp.arange(S)
    cT = None if c is None else jnp.moveaxis(c, 1, -1)

    def block(i):
        q0 = i * Q
        qb = lax.dynamic_slice_in_dim(q, q0, Q, axis=1)
        s = jnp.einsum('bqkgd,bskd->bkgqs', qb, k, preferred_element_type=jnp.float32) * scale
        if c is not None:
            cq = lax.dynamic_slice_in_dim(cT, q0, Q, axis=3)
            s = s + cq[..., :, None] - cT[..., None, :]
        qpos = q0 + jnp.arange(Q)
        s = jnp.where(kpos[None, :] <= qpos[:, None], s, NEG_INF)
        p = jax.nn.softmax(s, axis=-1).astype(v.dtype)
        return jnp.einsum('bkgqs,bskd->bqkgd', p, v)

    o = lax.map(block, jnp.arange(S // Q))
    return jnp.moveaxis(o, 0, 1).reshape(B, S, KH * G, v.shape[-1])


def fox_decode(q, k_new, v_new, logf_new, k_past, v_past, logf_past):
    B, T, KH, G, D = q.shape
    P = k_past.shape[1]
    scale = D ** -0.5
    lp = logf_past.astype(jnp.float32)
    suffix = lax.cumsum(lp, axis=1, reverse=True) - lp
    cn = jnp.cumsum(logf_new.astype(jnp.float32), axis=1)
    cnT = jnp.transpose(cn.reshape(B, T, KH, G), (0, 2, 3, 1))
    sufT = jnp.transpose(suffix.reshape(B, P, KH, G), (0, 2, 3, 1))
    s_past = (jnp.einsum('btkgd,bpkd->bkgtp', q, k_past, preferred_element_type=jnp.float32) * scale
              + cnT[..., :, None] + sufT[..., None, :])
    s_new = (jnp.einsum('btkgd,bskd->bkgts', q, k_new, preferred_element_type=jnp.float32) * scale
             + cnT[..., :, None] - cnT[..., None, :])
    s_new = jnp.where(causal_mask(T), s_new, NEG_INF)
    p_past, p_new = softmax_parts([s_past, s_new])
    return (jnp.einsum('bkgtp,bpkd->btkgd', p_past.astype(v_past.dtype), v_past)
            + jnp.einsum('bkgts,bskd->btkgd', p_new.astype(v_new.dtype), v_new))


def mla_decode(q_nope, q_rope, ckv_new, kr_new, ckv_past, kr_past, w_uk, w_uv):
    T = q_nope.shape[1]
    scale = (MLA_NOPE + MLA_ROPE) ** -0.5
    q_lat = jnp.einsum('bthn,chn->bthc', q_nope, w_uk)
    s_past = (jnp.einsum('bthc,bpc->bhtp', q_lat, ckv_past, preferred_element_type=jnp.float32)
              + jnp.einsum('bthr,bpr->bhtp', q_rope, kr_past, preferred_element_type=jnp.float32)) * scale
    s_new = (jnp.einsum('bthc,bsc->bhts', q_lat, ckv_new, preferred_element_type=jnp.float32)
             + jnp.einsum('bthr,bsr->bhts', q_rope, kr_new, preferred_element_type=jnp.float32)) * scale
    s_new = jnp.where(causal_mask(T), s_new, NEG_INF)
    p_past, p_new = softmax_parts([s_past, s_new])
    o_lat = (jnp.einsum('bhtp,bpc->bthc', p_past.astype(ckv_past.dtype), ckv_past)
             + jnp.einsum('bhts,bsc->bthc', p_new.astype(ckv_new.dtype), ckv_new))
    return jnp.einsum('bthc,chv->bthv', o_lat, w_uv)


def moba_prompt(q, k, v):
    B, S, KH, G, D = q.shape
    nb = -(-S // MOBA_BLOCK)
    pad = nb * MOBA_BLOCK - S
    kp = jnp.pad(k, ((0, 0), (0, pad), (0, 0), (0, 0)))
    vp = jnp.pad(v, ((0, 0), (0, pad), (0, 0), (0, 0)))
    kb = kp.reshape(B, nb, MOBA_BLOCK, KH, D)
    vb = vp.reshape(B, nb, MOBA_BLOCK, KH, D)
    n_sel = min(MOBA_TOPK, nb - 1)
    scale = D ** -0.5
    Qb = MOBA_Q_BLOCK
    if n_sel > 0:
        kmean = jnp.mean(kb.astype(jnp.float32), axis=2)
        gate = jnp.einsum('bskgd,bnkd->bskgn', q.astype(jnp.float32), kmean)
        own = jnp.arange(S) // MOBA_BLOCK
        fully_past = jnp.arange(nb)[None, :] < own[:, None]
        gate = jnp.where(fully_past[None, :, None, None, :], gate, NEG_INF)
        top_val, top_idx = lax.top_k(gate, n_sel)
        top_ok = top_val > 0.5 * NEG_INF
        bi = jnp.arange(B)[:, None, None, None, None]
        hi = jnp.arange(KH)[None, None, :, None, None]

    def block(i):
        q0 = i * Qb
        qb = lax.dynamic_slice_in_dim(q, q0, Qb, axis=1)
        b0 = (q0 // MOBA_BLOCK) * MOBA_BLOCK
        k_own = lax.dynamic_slice_in_dim(kp, b0, MOBA_BLOCK, axis=1)
        v_own = lax.dynamic_slice_in_dim(vp, b0, MOBA_BLOCK, axis=1)
        qpos = q0 + jnp.arange(Qb)
        kpos = b0 + jnp.arange(MOBA_BLOCK)
        s_own = jnp.einsum('bqkgd,blkd->bkgql', qb, k_own, preferred_element_type=jnp.float32) * scale
        s_own = jnp.where(kpos[None, :] <= qpos[:, None], s_own, NEG_INF)
        if n_sel == 0:
            p_own = jax.nn.softmax(s_own, axis=-1).astype(v.dtype)
            return jnp.einsum('bkgql,blkd->bqkgd', p_own, v_own)
        idx = lax.dynamic_slice_in_dim(top_idx, q0, Qb, axis=1)
        ok = lax.dynamic_slice_in_dim(top_ok, q0, Qb, axis=1)
        k_sel = kb[bi, idx, :, hi, :]
        v_sel = vb[bi, idx, :, hi, :]
        s_sel = jnp.einsum('bqkgd,bqkgjld->bkgqjl', qb, k_sel, preferred_element_type=jnp.float32) * scale
        s_sel = jnp.where(jnp.moveaxis(ok, 1, 3)[..., None], s_sel, NEG_INF)
        p_own, p_sel = softmax_parts([s_own, s_sel.reshape(B, KH, G, Qb, n_sel * MOBA_BLOCK)])
        p_sel = p_sel.reshape(B, KH, G, Qb, n_sel, MOBA_BLOCK).astype(v.dtype)
        return (jnp.einsum('bkgql,blkd->bqkgd', p_own.astype(v.dtype), v_own)
                + jnp.einsum('bkgqjl,bqkgjld->bqkgd', p_sel, v_sel))

    o = lax.map(block, jnp.arange(S // Qb))
    return jnp.moveaxis(o, 0, 1).reshape(B, S, KH * G * D)


def moba_decode(q, k_new, v_new, k_past, v_past):
    B, T, KH, G, D = q.shape
    P = k_past.shape[1]
    n_full = P // MOBA_BLOCK
    r0 = n_full * MOBA_BLOCK
    scale = D ** -0.5
    s_new = jnp.einsum('btkgd,bskd->bkgts', q, k_new, preferred_element_type=jnp.float32) * scale
    parts = [jnp.where(causal_mask(T), s_new, NEG_INF)]
    if r0 < P:
        k_own, v_own = k_past[:, r0:], v_past[:, r0:]
        parts.append(jnp.einsum('btkgd,bpkd->bkgtp', q, k_own, preferred_element_type=jnp.float32) * scale)
    if n_full > 0:
        n_sel = min(MOBA_TOPK, n_full)
        kb = k_past[:, :r0].reshape(B, n_full, MOBA_BLOCK, KH, D)
        vb = v_past[:, :r0].reshape(B, n_full, MOBA_BLOCK, KH, D)
        kmean = jnp.mean(kb.astype(jnp.float32), axis=2)
        gate = jnp.einsum('btkgd,bnkd->btkgn', q.astype(jnp.float32), kmean)
        _, idx = lax.top_k(gate, n_sel)
        bi = jnp.arange(B)[:, None, None, None, None]
        hi = jnp.arange(KH)[None, None, :, None, None]
        k_sel = kb[bi, idx, :, hi, :]
        v_sel = vb[bi, idx, :, hi, :]
        s_sel = jnp.einsum('btkgd,btkgjld->bkgtjl', q, k_sel, preferred_element_type=jnp.float32) * scale
        parts.append(s_sel.reshape(B, KH, G, T, n_sel * MOBA_BLOCK))
    probs = softmax_parts(parts)
    o = jnp.einsum('bkgts,bskd->btkgd', probs[0].astype(v_new.dtype), v_new)
    j = 1
    if r0 < P:
        o = o + jnp.einsum('bkgtp,bpkd->btkgd', probs[j].astype(v_own.dtype), v_own)
        j += 1
    if n_full > 0:
        p_sel = probs[j].reshape(B, KH, G, T, n_sel, MOBA_BLOCK).astype(v_sel.dtype)
        o = o + jnp.einsum('bkgtjl,btkgjld->btkgd', p_sel, v_sel)
    return o


def mix_layer(h, pos, w_in, w_gate, g_av, w_s, b_s, fox_bf, g_cq, g_ckv, w_uq, w_uk, w_uv, w_br, w_o, past):
    B, S, _ = h.shape
    dt = h.dtype
    z = jnp.einsum('bsd,de->bse', h, w_in)
    a_u, a_v, fq, fk, fv, ff, cq, ckv, kr, mq, mk, mv = split_in(z)
    a_v = rmsnorm(a_v.reshape(B, S, A_GROUPS, A_DIM), g_av)
    y_a = spatial_gate(a_u.reshape(B, S, A_GROUPS, A_DIM), a_v, w_s, b_s)
    fq = fq.reshape(B, S, FOX_KV_HEADS, FOX_GROUP, HEAD_DIM)
    fk = fk.reshape(B, S, FOX_KV_HEADS, HEAD_DIM)
    fv = fv.reshape(B, S, FOX_KV_HEADS, HEAD_DIM)
    logf = jax.nn.log_sigmoid(ff.astype(jnp.float32) + fox_bf.astype(jnp.float32))
    cq = rmsnorm(cq, g_cq)
    ckv = rmsnorm(ckv, g_ckv)
    qc = jnp.einsum('bsc,chd->bshd', cq, w_uq)
    q_nope = qc[..., :MLA_NOPE]
    q_rope = rope(qc[..., MLA_NOPE:], pos, MLA_ROPE)
    kr = rope(kr[:, :, None, :], pos, MLA_ROPE)[:, :, 0, :]
    mq = rope(mq.reshape(B, S, MOBA_HEADS, HEAD_DIM), pos, ROT_DIM).reshape(B, S, MOBA_KV_HEADS, MOBA_GROUP, HEAD_DIM)
    mk = rope(mk.reshape(B, S, MOBA_KV_HEADS, HEAD_DIM), pos, ROT_DIM)
    mv = mv.reshape(B, S, MOBA_KV_HEADS, HEAD_DIM)
    if past is None:
        c = jnp.cumsum(logf, axis=1).reshape(B, S, FOX_KV_HEADS, FOX_GROUP)
        y_b = blocked_causal_attention(fq, fk, fv, c)
        k_c = jnp.concatenate([jnp.einsum('bsc,chd->bshd', ckv, w_uk),
                               jnp.broadcast_to(kr[:, :, None, :], (B, S, MLA_HEADS, MLA_ROPE))], axis=-1)
        v_c = jnp.einsum('bsc,chd->bshd', ckv, w_uv)
        q_c = jnp.concatenate([q_nope, q_rope], axis=-1)[:, :, :, None, :]
        y_c = blocked_causal_attention(q_c, k_c, v_c)
        y_d = moba_prompt(mq, mk, mv)
    else:
        fk_p, fv_p, logf_p, ckv_p, kr_p, mk_p, mv_p = past
        y_b = fox_decode(fq, fk, fv, logf, fk_p, fv_p, logf_p)
        y_c = mla_decode(q_nope, q_rope, ckv, kr, ckv_p, kr_p, w_uk, w_uv)
        y_d = moba_decode(mq, mk, mv, mk_p, mv_p)
    merged = [jax.nn.sigmoid(jnp.einsum('bsd,de->bse', h, w_gate[i]))
              * jnp.einsum('bsw,wd->bsd', y.reshape(B, S, BR_WIDTH).astype(dt), w_br[i])
              for i, y in enumerate((y_a, y_b, y_c, y_d))]
    out = jnp.einsum('bsd,de->bse', merged[0] + merged[1] + merged[2] + merged[3], w_o)
    return out, (fk, fv, logf, ckv, kr, mk, mv, a_v)


def sq_relu_mlp(h, w_up, w_down):
    return jnp.einsum('bsf,fd->bsd', jnp.square(jax.nn.relu(jnp.einsum('bsd,df->bsf', h, w_up))), w_down)


def gather_pages(cache, layer, page_table):
    g = cache[layer, page_table]
    return g.reshape((g.shape[0], g.shape[1] * g.shape[2]) + g.shape[3:])


def run_trunk(x, pos, caches, page_table, weights):
    (g_attn, w_in, w_gate, g_av, w_s, b_s, fox_bf, g_cq, g_ckv, w_uq, w_uk, w_uv,
     w_br, w_o, g_mlp, w_up, w_down, g_final) = weights
    rows = []
    for l in range(DEPTH):
        past = None if caches is None else tuple(gather_pages(c, l, page_table) for c in caches)
        mix, new_rows = mix_layer(rmsnorm(x, g_attn[l]), pos, w_in[l], w_gate[l], g_av[l], w_s[l], b_s[l],
                                  fox_bf[l], g_cq[l], g_ckv[l], w_uq[l], w_uk[l], w_uv[l], w_br[l], w_o[l], past)
        x = x + mix
        x = x + sq_relu_mlp(rmsnorm(x, g_mlp[l]), w_up[l], w_down[l])
        rows.append(new_rows)
    y = rmsnorm(x, g_final)
    return y, tuple(jnp.stack(r, axis=0) for r in zip(*rows))


def setup_inputs(seed: int = 0) -> dict:
    key = jax.random.key(seed)
    ks = iter(jax.random.split(key, 40))

    def nrm(shape, scale=1.0):
        return scale * jax.random.normal(next(ks), shape, jnp.float32)

    n_pages = PAST_LEN // PAGE_SIZE
    n_used = DEC_BATCH * n_pages
    n_pool = (5 * n_used + 3) // 4
    x_prompt = nrm((BATCH, SEQ, D_MODEL))
    x_sample = nrm((DEC_BATCH, DEC_SEQ, D_MODEL))
    cache_fox_k = nrm((DEPTH, n_pool, PAGE_SIZE, FOX_KV_HEADS, HEAD_DIM))
    cache_fox_v = nrm((DEPTH, n_pool, PAGE_SIZE, FOX_KV_HEADS, HEAD_DIM))
    cache_fox_logf = jax.nn.log_sigmoid(FOX_BIAS_CENTER + nrm((DEPTH, n_pool, PAGE_SIZE, FOX_HEADS)))
    cache_mla_ckv = nrm((DEPTH, n_pool, PAGE_SIZE, MLA_KV_LORA))
    cache_mla_krope = nrm((DEPTH, n_pool, PAGE_SIZE, MLA_ROPE))
    cache_moba_k = nrm((DEPTH, n_pool, PAGE_SIZE, MOBA_KV_HEADS, HEAD_DIM))
    cache_moba_v = nrm((DEPTH, n_pool, PAGE_SIZE, MOBA_KV_HEADS, HEAD_DIM))
    page_table = jax.random.permutation(next(ks), n_pool)[:n_used].reshape(DEC_BATCH, n_pages).astype(jnp.int32)
    return {
        'x_prompt': x_prompt, 'x_sample': x_sample,
        'cache_fox_k': cache_fox_k, 'cache_fox_v': cache_fox_v, 'cache_fox_logf': cache_fox_logf,
        'cache_mla_ckv': cache_mla_ckv, 'cache_mla_krope': cache_mla_krope,
        'cache_moba_k': cache_moba_k, 'cache_moba_v': cache_moba_v,
        'page_table': page_table,
        'g_attn': 1.0 + nrm((DEPTH, D_MODEL), 0.05),
        'w_in': nrm((DEPTH, D_MODEL, D_IN), D_MODEL ** -0.5),
        'w_gate': nrm((DEPTH, N_BRANCH, D_MODEL, D_MODEL), D_MODEL ** -0.5),
        'g_av': 1.0 + nrm((DEPTH, A_GROUPS, A_DIM), 0.05),
        'w_s': nrm((DEPTH, A_GROUPS, CHUNK, CHUNK), CHUNK ** -0.5),
        'b_s': 1.0 + nrm((DEPTH, A_GROUPS, CHUNK), 0.1),
        'fox_bf': FOX_BIAS_CENTER + nrm((DEPTH, FOX_HEADS), 0.1),
        'g_cq': 1.0 + nrm((DEPTH, MLA_Q_LORA), 0.05),
        'g_ckv': 1.0 + nrm((DEPTH, MLA_KV_LORA), 0.05),
        'w_uq': nrm((DEPTH, MLA_Q_LORA, MLA_HEADS, MLA_NOPE + MLA_ROPE), MLA_Q_LORA ** -0.5),
        'w_uk': nrm((DEPTH, MLA_KV_LORA, MLA_HEADS, MLA_NOPE), MLA_KV_LORA ** -0.5),
        'w_uv': nrm((DEPTH, MLA_KV_LORA, MLA_HEADS, MLA_V), MLA_KV_LORA ** -0.5),
        'w_br': nrm((DEPTH, N_BRANCH, BR_WIDTH, D_MODEL), BR_WIDTH ** -0.5),
        'w_o': nrm((DEPTH, D_MODEL, D_MODEL), D_MODEL ** -0.5),
        'g_mlp': 1.0 + nrm((DEPTH, D_MODEL), 0.05),
        'w_up': nrm((DEPTH, D_MODEL, D_FF), D_MODEL ** -0.5),
        'w_down': nrm((DEPTH, D_FF, D_MODEL), D_FF ** -0.5),
        'g_final': 1.0 + nrm((D_MODEL,), 0.05),
    }


def reference(x_prompt, x_sample, cache_fox_k, cache_fox_v, cache_fox_logf, cache_mla_ckv, cache_mla_krope,
              cache_moba_k, cache_moba_v, page_table, g_attn, w_in, w_gate, g_av, w_s, b_s, fox_bf, g_cq, g_ckv,
              w_uq, w_uk, w_uv, w_br, w_o, g_mlp, w_up, w_down, g_final):
    weights = (g_attn, w_in, w_gate, g_av, w_s, b_s, fox_bf, g_cq, g_ckv, w_uq, w_uk, w_uv,
               w_br, w_o, g_mlp, w_up, w_down, g_final)
    pos_prompt = jnp.arange(x_prompt.shape[1])
    past_len = page_table.shape[1] * PAGE_SIZE
    pos_sample = past_len + jnp.arange(x_sample.shape[1])
    y_prompt, p_rows = run_trunk(x_prompt, pos_prompt, None, None, weights)
    caches = (cache_fox_k, cache_fox_v, cache_fox_logf, cache_mla_ckv, cache_mla_krope, cache_moba_k, cache_moba_v)
    y_sample, s_rows = run_trunk(x_sample, pos_sample, caches, page_table, weights)
    p_fox_k, p_fox_v, p_fox_logf, p_mla_ckv, p_mla_krope, p_moba_k, p_moba_v, _ = p_rows
    s_fox_k, s_fox_v, s_fox_logf, s_mla_ckv, s_mla_krope, s_moba_k, s_moba_v, s_chunk_v = s_rows
    return (y_prompt, y_sample,
            p_fox_k, p_fox_v, p_fox_logf, p_mla_ckv, p_mla_krope, p_moba_k, p_moba_v,
            s_fox_k, s_fox_v, s_fox_logf, s_mla_ckv, s_mla_krope, s_moba_k, s_moba_v, s_chunk_v)
```

```python
import functools

import numpy as np
import jax
import jax.numpy as jnp
from jax import lax
from jax.experimental import pallas as pl
from jax.experimental.pallas import tpu as pltpu

F32 = jnp.float32
BF16 = jnp.bfloat16

D_MODEL = 1024
HEAD_DIM = 64
BR_WIDTH = 256
ROPE_THETA = 500000.0
NORM_EPS = 1e-6
NEG_INF = -1e30
D_FF = 4 * D_MODEL
CHUNK = 128
A_DIM = 64
FOX_HEADS = 4
MLA_HEADS = 4
MLA_NOPE = 64
MLA_ROPE = 32
MLA_KV_LORA = 128
MLA_Q_LORA = 256
MOBA_BLOCK = 256
MOBA_TOPK = 3
ROT_DIM = 16
PAGE_SIZE = 128

LANES = 128
D_INP = 16 * LANES
HEAD_ORDER = (0, 3, 1, 2)
FOX_SCALE = HEAD_DIM ** -0.5
MLA_SCALE = (MLA_NOPE + MLA_ROPE) ** -0.5
LOGF_LANE = 32
VMEM_LIMIT = 56 * 1024 * 1024


def _cparams(*sem):
    return pltpu.CompilerParams(dimension_semantics=sem, vmem_limit_bytes=VMEM_LIMIT)


def _rms(x, g):
    return x * lax.rsqrt(jnp.mean(x * x, axis=-1, keepdims=True) + NORM_EPS) * g


def _split3(a):
    a1 = a.astype(BF16)
    r1 = a - a1.astype(F32)
    a2 = r1.astype(BF16)
    a3 = (r1 - a2.astype(F32)).astype(BF16)
    return a1, a2, a3


def _dot(a, b):
    return jnp.dot(a, b, preferred_element_type=F32)


def _dot_nt(a, b):
    return lax.dot_general(a, b, (((1,), (1,)), ((), ())), preferred_element_type=F32)


def _dot3_lhs(a_f32, b_bf16):
    a1, a2, a3 = _split3(a_f32)
    return _dot(a1, b_bf16) + _dot(a2, b_bf16) + _dot(a3, b_bf16)


def _dot3_rhs(a_bf16, b_f32):
    b1, b2, b3 = _split3(b_f32)
    return _dot(a_bf16, b1) + _dot(a_bf16, b2) + _dot(a_bf16, b3)


def _dot3_nt_rhs(a_bf16, b_f32):
    b1, b2, b3 = _split3(b_f32)
    return _dot_nt(a_bf16, b1) + _dot_nt(a_bf16, b2) + _dot_nt(a_bf16, b3)


def _rope(x, cos, sin, half, period):
    lane = lax.broadcasted_iota(jnp.int32, x.shape, 1) % period
    partner = jnp.where(lane < half, pltpu.roll(x, LANES - half, 1), pltpu.roll(x, half, 1))
    return x * cos + partner * sin


def _log_sigmoid(x):
    return jnp.minimum(x, 0.0) - jnp.log1p(jnp.exp(-jnp.abs(x)))


def _front_body(x_ref, ga_ref, win_ref, gav_ref, gmat_ref, gcq_ref, gckv_ref, wuq_ref, wuk_ref, bf_ref,
                cq_ref, sq_ref, cm_ref, sm_ref,
                au_ref, avn_ref, fq_ref, fk_ref, fv_ref, misc_ref, ckvn_ref, qc_ref, mq_ref, mk_ref, mv_ref):
    x = x_ref[...]
    h = _rms(x, ga_ref[...]).astype(BF16)
    z = _dot(h, win_ref[...])
    au_ref[...] = z[:, 0:256]
    av = z[:, 256:512]
    ms = _dot3_lhs(av * av, gmat_ref[...]) * (1.0 / A_DIM)
    avn_ref[...] = av * lax.rsqrt(ms + NORM_EPS) * gav_ref[...]
    fq_ref[...] = (z[:, 512:768] * FOX_SCALE).astype(BF16)
    fk_ref[...] = z[:, 768:896]
    fv_ref[...] = z[:, 896:1024]
    cos_q, sin_q = cq_ref[...], sq_ref[...]
    cqn = _rms(z[:, 1024:1280], gcq_ref[...]).astype(BF16)
    ckvn_ref[...] = _rms(z[:, 1280:1408], gckv_ref[...])
    qc = _dot(cqn, wuq_ref[...])
    qlat = _dot(qc[:, 0:256].astype(BF16), wuk_ref[...])
    for hd in range(MLA_HEADS):
        qc_ref[:, 256 * hd:256 * hd + 128] = (qlat[:, 128 * hd:128 * hd + 128] * MLA_SCALE).astype(BF16)
        qr = _rope(qc[:, 256 + 128 * hd:384 + 128 * hd], cos_q, sin_q, MLA_ROPE // 2, LANES)
        qc_ref[:, 256 * hd + 128:256 * hd + 256] = (qr * MLA_SCALE).astype(BF16)
    mb = z[:, 1408:1536]
    lane = lax.broadcasted_iota(jnp.int32, mb.shape, 1)
    kr = _rope(mb, cos_q, sin_q, MLA_ROPE // 2, LANES)
    logf = _log_sigmoid(mb + bf_ref[...])
    misc_ref[...] = jnp.where(lane < MLA_ROPE, kr, jnp.where(lane < LOGF_LANE + FOX_HEADS, logf, 0.0))
    cos_m, sin_m = cm_ref[...], sm_ref[...]
    for blk in range(2):
        mq = _rope(z[:, 1536 + 128 * blk:1664 + 128 * blk], cos_m, sin_m, ROT_DIM // 2, HEAD_DIM)
        mq_ref[:, 128 * blk:128 * blk + 128] = (mq * FOX_SCALE).astype(BF16)
    mk_ref[...] = _rope(z[:, 1792:1920], cos_m, sin_m, ROT_DIM // 2, HEAD_DIM)
    mv_ref[...] = z[:, 1920:2048]


def _front(x, lw, tabs, tm, tab_blocks):
    m = x.shape[0]
    grid = (m // tm,)
    row = lambda w: pl.BlockSpec((tm, w), lambda i: (i, 0))
    full = lambda a: pl.BlockSpec(a.shape, lambda i: (0,) * a.ndim)
    tab = pl.BlockSpec((tm, LANES), lambda i: (i % tab_blocks, 0))
    widths = (256, 256, 256, 128, 128, 128, 128, 1024, 256, 128, 128)
    dtypes = (F32, F32, BF16, F32, F32, F32, F32, BF16, BF16, F32, F32)
    consts = (lw["g_attn"], lw["w_in"], lw["g_av"], lw["gmat"], lw["g_cq"], lw["g_ckv"], lw["w_uq"], lw["w_uk_bd"],
              lw["bf_vec"])
    return pl.pallas_call(
        _front_body,
        grid=grid,
        in_specs=[row(D_MODEL)] + [full(a) for a in consts] + [tab] * 4,
        out_specs=[row(w) for w in widths],
        out_shape=[jax.ShapeDtypeStruct((m, w), d) for w, d in zip(widths, dtypes)],
        compiler_params=_cparams("parallel"),
        name="front",
    )(x, *consts, *tabs)


def _prep_body(misc_ref, mk_ref, ccol_ref, crow_ref, kmean_ref):
    s = misc_ref.shape[0]
    r = lax.broadcasted_iota(jnp.int32, (LANES, LANES), 0)
    c = lax.broadcasted_iota(jnp.int32, (LANES, LANES), 1)
    tri = jnp.where(r >= c, 1.0, 0.0).astype(BF16)
    er = lax.broadcasted_iota(jnp.int32, (8, LANES), 0)
    ec = lax.broadcasted_iota(jnp.int32, (8, LANES), 1)
    pick = jnp.where(ec == er + LOGF_LANE, 1.0, 0.0).astype(BF16)
    carry = jnp.zeros((1, LANES), F32)
    for t in range(s // LANES):
        tile = misc_ref[t * LANES:(t + 1) * LANES, :]
        cs = _dot3_rhs(tri, tile) + carry
        ccol_ref[t * LANES:(t + 1) * LANES, :] = cs
        crow_ref[:, t * LANES:(t + 1) * LANES] = _dot3_nt_rhs(pick, cs)
        carry = cs[LANES - 1:LANES, :]
    nb = s // MOBA_BLOCK
    km = jnp.sum(mk_ref[...].reshape(nb, MOBA_BLOCK, LANES), axis=1) * (1.0 / MOBA_BLOCK)
    kmean_ref[...] = jnp.zeros(kmean_ref.shape, F32)
    kmean_ref[0:nb, :] = km


def _prep(misc, mk, batch, seq):
    return pl.pallas_call(
        _prep_body,
        grid=(batch,),
        in_specs=[pl.BlockSpec((seq, LANES), lambda b: (b, 0))] * 2,
        out_specs=[pl.BlockSpec((seq, LANES), lambda b: (b, 0)),
                   pl.BlockSpec((None, 8, seq), lambda b: (b, 0, 0)),
                   pl.BlockSpec((None, LANES, LANES), lambda b: (b, 0, 0))],
        out_shape=[jax.ShapeDtypeStruct((batch * seq, LANES), F32),
                   jax.ShapeDtypeStruct((batch, 8, seq), F32),
                   jax.ShapeDtypeStruct((batch, LANES, LANES), F32)],
        compiler_params=_cparams("parallel"),
        name="prep",
    )(misc, mk)


def _spatial_body(au_ref, avn_ref, ws_ref, bias_ref, ya_ref):
    rows = au_ref.shape[0]
    r = lax.broadcasted_iota(jnp.int32, (CHUNK, CHUNK), 0)
    c = lax.broadcasted_iota(jnp.int32, (CHUNK, CHUNK), 1)
    grp = lax.broadcasted_iota(jnp.int32, (CHUNK, BR_WIDTH), 1) // A_DIM
    ws = [jnp.where(r >= c, ws_ref[g], 0.0).astype(BF16) for g in range(4)]
    for t in range(rows // CHUNK):
        sl = slice(t * CHUNK, (t + 1) * CHUNK)
        v = avn_ref[sl, :].astype(BF16)
        s = bias_ref[...]
        for g in range(4):
            s = s + jnp.where(grp == g, _dot(ws[g], v), 0.0)
        ya_ref[sl, :] = (au_ref[sl, :] * s).astype(BF16)


def _spatial(au, avn, w_s, bias, tm):
    m = au.shape[0]
    row = pl.BlockSpec((tm, BR_WIDTH), lambda i: (i, 0))
    return pl.pallas_call(
        _spatial_body,
        grid=(m // tm,),
        in_specs=[row, row, pl.BlockSpec(w_s.shape, lambda i: (0, 0, 0)), pl.BlockSpec(bias.shape, lambda i: (0, 0))],
        out_specs=row,
        out_shape=jax.ShapeDtypeStruct((m, BR_WIDTH), BF16),
        compiler_params=_cparams("parallel"),
        name="spatial",
    )(au, avn, w_s, bias)


def _spatial1_body(au_ref, avn_ref, w0_ref, b0_ref, ya_ref):
    ya_ref[...] = (au_ref[...] * (w0_ref[...] * avn_ref[...] + b0_ref[...])).astype(BF16)


def _spatial1(au, avn, w0, b0):
    m = au.shape[0]
    return pl.pallas_call(
        _spatial1_body,
        out_shape=jax.ShapeDtypeStruct((m, BR_WIDTH), BF16),
        name="spatial1",
    )(au, avn, w0, b0)


def _online(s, v, m_ref, l_ref, acc_ref, idx):
    m_prev = m_ref[idx]
    m_new = jnp.maximum(m_prev, jnp.max(s, axis=-1, keepdims=True))
    alpha = jnp.exp(m_prev - m_new)
    p = jnp.exp(s - m_new)
    l_ref[idx] = alpha * l_ref[idx] + jnp.sum(p, axis=-1, keepdims=True)
    acc_ref[idx] = alpha * acc_ref[idx] + _dot(p.astype(BF16), v)
    m_ref[idx] = m_new


def _init_stats(m_ref, l_ref, acc_ref):
    m_ref[...] = jnp.full(m_ref.shape, -jnp.inf, F32)
    l_ref[...] = jnp.zeros(l_ref.shape, F32)
    acc_ref[...] = jnp.zeros(acc_ref.shape, F32)


def _head_mask(qb, half):
    lane = lax.broadcasted_iota(jnp.int32, qb.shape, 1)
    keep = (lane < HEAD_DIM) if half == 0 else (lane >= HEAD_DIM)
    return jnp.where(keep, qb, jnp.zeros_like(qb))


def _merge_heads(acc_ref, l_ref, out_ref):
    lane = lax.broadcasted_iota(jnp.int32, acc_ref.shape[1:], 1)
    for blk in range(2):
        o0 = acc_ref[2 * blk] / l_ref[2 * blk]
        o1 = acc_ref[2 * blk + 1] / l_ref[2 * blk + 1]
        out_ref[:, 128 * blk:128 * blk + 128] = jnp.where(lane < HEAD_DIM, o0, o1).astype(out_ref.dtype)


def _fox_body(fq_ref, fk_ref, fv_ref, ccol_ref, crow_ref, yb_ref, m_ref, l_ref, acc_ref):
    i, j = pl.program_id(1), pl.program_id(2)
    tq, tk = fq_ref.shape[0], fk_ref.shape[0]

    @pl.when(j == 0)
    def _():
        _init_stats(m_ref, l_ref, acc_ref)

    def step(diag):
        k = fk_ref[...].astype(BF16)
        v = fv_ref[...].astype(BF16)
        if diag:
            causal = (lax.broadcasted_iota(jnp.int32, (tq, tk), 1) <= lax.broadcasted_iota(jnp.int32, (tq, tk), 0))
        for blk in range(2):
            qb = fq_ref[:, 128 * blk:128 * blk + 128]
            for half in range(2):
                idx = 2 * blk + half
                hd = HEAD_ORDER[idx]
                s = _dot_nt(_head_mask(qb, half), k)
                s = s + (ccol_ref[:, LOGF_LANE + hd:LOGF_LANE + hd + 1] - crow_ref[hd:hd + 1, :])
                if diag:
                    s = jnp.where(causal, s, NEG_INF)
                _online(s, v, m_ref, l_ref, acc_ref, idx)

    @pl.when(j < i)
    def _():
        step(False)

    @pl.when(j == i)
    def _():
        step(True)
        _merge_heads(acc_ref, l_ref, yb_ref)


def _fox_prompt(fq, fk, fv, ccol, crow, batch, seq, t):
    n = seq // t
    kv = lambda w: pl.BlockSpec((t, w), lambda b, i, j: (b * n + jnp.minimum(j, i), 0))
    qs = lambda w: pl.BlockSpec((t, w), lambda b, i, j: (b * n + i, 0))
    return pl.pallas_call(
        _fox_body,
        grid=(batch, n, n),
        in_specs=[qs(256), kv(128), kv(128), qs(128),
                  pl.BlockSpec((None, 8, t), lambda b, i, j: (b, 0, jnp.minimum(j, i)))],
        out_specs=qs(256),
        out_shape=jax.ShapeDtypeStruct((batch * seq, 256), BF16),
        scratch_shapes=[pltpu.VMEM((4, t, 1), F32), pltpu.VMEM((4, t, 1), F32), pltpu.VMEM((4, t, LANES), F32)],
        compiler_params=_cparams("parallel", "parallel", "arbitrary"),
        name="fox_prompt",
    )(fq, fk, fv, ccol, crow)


def _mla_body(qc_ref, ckvn_ref, misc_ref, yc_ref, m_ref, l_ref, acc_ref):
    i, j = pl.program_id(1), pl.program_id(2)
    tq, tk = qc_ref.shape[0], ckvn_ref.shape[0]

    @pl.when(j == 0)
    def _():
        _init_stats(m_ref, l_ref, acc_ref)

    def step(diag):
        v = ckvn_ref[...].astype(BF16)
        k = jnp.concatenate([v, misc_ref[...].astype(BF16)], axis=1)
        if diag:
            causal = (lax.broadcasted_iota(jnp.int32, (tq, tk), 1) <= lax.broadcasted_iota(jnp.int32, (tq, tk), 0))
        for hd in range(MLA_HEADS):
            s = _dot_nt(qc_ref[:, 256 * hd:256 * hd + 256], k)
            if diag:
                s = jnp.where(causal, s, NEG_INF)
            _online(s, v, m_ref, l_ref, acc_ref, hd)

    @pl.when(j < i)
    def _():
        step(False)

    @pl.when(j == i)
    def _():
        step(True)
        for hd in range(MLA_HEADS):
            yc_ref[:, 128 * hd:128 * hd + 128] = (acc_ref[hd] / l_ref[hd]).astype(BF16)


def _mla_prompt(qc, ckvn, misc, batch, seq, t):
    n = seq // t
    kv = lambda w: pl.BlockSpec((t, w), lambda b, i, j: (b * n + jnp.minimum(j, i), 0))
    qs = lambda w: pl.BlockSpec((t, w), lambda b, i, j: (b * n + i, 0))
    return pl.pallas_call(
        _mla_body,
        grid=(batch, n, n),
        in_specs=[qs(1024), kv(128), kv(128)],
        out_specs=qs(512),
        out_shape=jax.ShapeDtypeStruct((batch * seq, 512), BF16),
        scratch_shapes=[pltpu.VMEM((4, t, 1), F32), pltpu.VMEM((4, t, 1), F32), pltpu.VMEM((4, t, LANES), F32)],
        compiler_params=_cparams("parallel", "parallel", "arbitrary"),
        name="mla_prompt",
    )(qc, ckvn, misc)


def _topk_rank(g, n_cand):
    lane = lax.broadcasted_iota(jnp.int32, g.shape, 1)
    rank = jnp.zeros(g.shape, F32)
    for mth in range(n_cand):
        gm = g[:, mth:mth + 1]
        beats = (gm > g) | ((gm == g) & (mth < lane))
        rank = rank + jnp.where(beats, 1.0, 0.0)
    return rank


def _topk_mask(g, n_cand):
    return jnp.where((_topk_rank(g, n_cand) < MOBA_TOPK) & (g > 0.5 * NEG_INF), 1.0, 0.0)


def _moba_body(mq_ref, mk_ref, mv_ref, kmean_ref, yd_ref, m_ref, l_ref, acc_ref, sel_ref):
    i, t = pl.program_id(1), pl.program_id(2)
    tq, tk = mq_ref.shape[0], mk_ref.shape[0]

    def step(own):
        k = mk_ref[...].astype(BF16)
        v = mv_ref[...].astype(BF16)
        if own:
            keep = (lax.broadcasted_iota(jnp.int32, (tq, tk), 1) <= lax.broadcasted_iota(jnp.int32, (tq, tk), 0))
        lane = lax.broadcasted_iota(jnp.int32, (tq, LANES), 1)
        for blk in range(2):
            qb = mq_ref[:, 128 * blk:128 * blk + 128]
            for half in range(2):
                idx = 2 * blk + half
                qm = _head_mask(qb, half)
                if own:
                    gate = _dot3_nt_rhs(qm, kmean_ref[...]) * (1.0 / FOX_SCALE)
                    gate = jnp.where(lane < i, gate, NEG_INF)
                    sel_ref[idx] = _topk_mask(gate, pl.num_programs(1))
                    mask = keep
                else:
                    picked = jnp.max(jnp.where(lane == t - 1, sel_ref[idx], 0.0), axis=-1, keepdims=True)
                    mask = picked > 0.0
                s = jnp.where(mask, _dot_nt(qm, k), NEG_INF)
                _online(s, v, m_ref, l_ref, acc_ref, idx)

    @pl.when(t == 0)
    def _():
        _init_stats(m_ref, l_ref, acc_ref)
        step(True)

    @pl.when((t > 0) & (t <= i))
    def _():
        step(False)

    @pl.when(t == pl.num_programs(2) - 1)
    def _():
        _merge_heads(acc_ref, l_ref, yd_ref)


def _moba_prompt(mq, mk, mv, kmean, batch, seq):
    t = MOBA_BLOCK
    n = seq // t
    kv_idx = lambda b, i, s: (b * n + jnp.where(s == 0, i, jnp.minimum(s - 1, jnp.maximum(i - 1, 0))), 0)
    kv = lambda w: pl.BlockSpec((t, w), kv_idx)
    qs = lambda w: pl.BlockSpec((t, w), lambda b, i, s: (b * n + i, 0))
    return pl.pallas_call(
        _moba_body,
        grid=(batch, n, n),
        in_specs=[qs(256), kv(128), kv(128), pl.BlockSpec((None, LANES, LANES), lambda b, i, s: (b, 0, 0))],
        out_specs=qs(256),
        out_shape=jax.ShapeDtypeStruct((batch * seq, 256), BF16),
        scratch_shapes=[pltpu.VMEM((4, t, 1), F32), pltpu.VMEM((4, t, 1), F32), pltpu.VMEM((4, t, LANES), F32),
                        pltpu.VMEM((4, t, LANES), F32)],
        compiler_params=_cparams("parallel", "parallel", "arbitrary"),
        name="moba_prompt",
    )(mq, mk, mv, kmean)


def _mix_body(x_ref, ga_ref, ya_ref, yb_ref, ycl_ref, yd_ref, wg_ref, wbr_ref, wuv_ref, wo_ref, x1_ref):
    x = x_ref[...]
    h = _rms(x, ga_ref[...]).astype(BF16)
    yc = _dot(ycl_ref[...], wuv_ref[...]).astype(BF16)
    ys = (ya_ref[...], yb_ref[...], yc, yd_ref[...])
    merged = None
    for b in range(4):
        term = jax.nn.sigmoid(_dot(h, wg_ref[b])) * _dot(ys[b], wbr_ref[b])
        merged = term if merged is None else merged + term
    x1_ref[...] = x + _dot(merged.astype(BF16), wo_ref[...])


def _mix(x, ya, yb, ycl, yd, lw, tm):
    m = x.shape[0]
    row = lambda w: pl.BlockSpec((tm, w), lambda i: (i, 0))
    full = lambda a: pl.BlockSpec(a.shape, lambda i: (0,) * a.ndim)
    consts = (lw["w_gate"], lw["w_br"], lw["w_uv_bd"], lw["w_o"])
    return pl.pallas_call(
        _mix_body,
        grid=(m // tm,),
        in_specs=[row(D_MODEL), full(lw["g_attn"]), row(256), row(256), row(512), row(256)] + [full(a) for a in consts],
        out_specs=row(D_MODEL),
        out_shape=jax.ShapeDtypeStruct((m, D_MODEL), F32),
        compiler_params=_cparams("parallel"),
        name="mix",
    )(x, lw["g_attn"], ya, yb, ycl, yd, *consts)


def _mlp_body(x_ref, g_ref, wup_ref, wdn_ref, gf_ref, o_ref, *, final):
    x = x_ref[...]
    h = _rms(x, g_ref[...]).astype(BF16)
    acc = x
    fc = 1024
    for c in range(D_FF // fc):
        u = jnp.maximum(_dot(h, wup_ref[:, c * fc:(c + 1) * fc]), 0.0)
        acc = acc + _dot((u * u).astype(BF16), wdn_ref[c * fc:(c + 1) * fc, :])
    o_ref[...] = _rms(acc, gf_ref[...]) if final else acc


def _mlp(x, lw, g_final, final, tm):
    m = x.shape[0]
    row = pl.BlockSpec((tm, D_MODEL), lambda i: (i, 0))
    full = lambda a: pl.BlockSpec(a.shape, lambda i: (0,) * a.ndim)
    consts = (lw["g_mlp"], lw["w_up"], lw["w_down"], g_final)
    return pl.pallas_call(
        functools.partial(_mlp_body, final=final),
        grid=(m // tm,),
        in_specs=[row] + [full(a) for a in consts],
        out_specs=row,
        out_shape=jax.ShapeDtypeStruct((m, D_MODEL), F32),
        compiler_params=_cparams("parallel"),
        name="mlp",
    )(x, *consts)


def _perm_heads(w, axis):
    parts = [lax.slice_in_dim(w, h * HEAD_DIM, (h + 1) * HEAD_DIM, axis=axis) for h in HEAD_ORDER]
    return jnp.concatenate(parts, axis=axis)


def _layer_weights(l, g_attn, w_in, w_gate, g_av, w_s, b_s, fox_bf, g_cq, g_ckv, w_uq, w_uk, w_uv, w_br, w_o,
                   g_mlp, w_up, w_down):
    wi = w_in[l]
    a_u, a_v, fq, fk, fv = wi[:, 0:256], wi[:, 256:512], wi[:, 512:768], wi[:, 768:896], wi[:, 896:1024]
    ff, cq, ckv, kr = wi[:, 1024:1028], wi[:, 1028:1284], wi[:, 1284:1412], wi[:, 1412:1444]
    mq, mk, mv = wi[:, 1444:1700], wi[:, 1700:1828], wi[:, 1828:1956]
    misc = jnp.concatenate([kr, ff, jnp.zeros((D_MODEL, LANES - MLA_ROPE - FOX_HEADS), F32)], axis=1)
    w_in_p = jnp.concatenate([a_u, a_v, _perm_heads(fq, 1), fk, fv, cq, ckv, misc, _perm_heads(mq, 1), mk, mv], axis=1)
    uq = w_uq[l]
    zpad = jnp.zeros((MLA_Q_LORA, LANES - MLA_ROPE), F32)
    uq_p = jnp.concatenate([uq[:, :, :MLA_NOPE].reshape(MLA_Q_LORA, MLA_HEADS * MLA_NOPE)]
                           + [a for h in range(MLA_HEADS) for a in (uq[:, h, MLA_NOPE:], zpad)], axis=1)
    uk_bd = jnp.zeros((MLA_HEADS * MLA_NOPE, MLA_HEADS * MLA_KV_LORA), F32)
    uv_bd = jnp.zeros((MLA_HEADS * MLA_KV_LORA, BR_WIDTH), F32)
    for h in range(MLA_HEADS):
        uk_bd = uk_bd.at[64 * h:64 * h + 64, 128 * h:128 * h + 128].set(w_uk[l][:, h, :].T)
        uv_bd = uv_bd.at[128 * h:128 * h + 128, 64 * h:64 * h + 64].set(w_uv[l][:, h, :])
    br = jnp.stack([w_br[l, 0], _perm_heads(w_br[l, 1], 0), w_br[l, 2], _perm_heads(w_br[l, 3], 0)], axis=0)
    grp = np.arange(BR_WIDTH) // A_DIM
    return {
        "g_attn": g_attn[l][None, :], "w_in": w_in_p.astype(BF16),
        "g_av": g_av[l].reshape(1, BR_WIDTH), "gmat": jnp.asarray(grp[:, None] == grp[None, :], BF16),
        "g_cq": g_cq[l][None, :], "g_ckv": g_ckv[l][None, :],
        "w_uq": uq_p.astype(BF16), "w_uk_bd": uk_bd.astype(BF16), "w_uv_bd": uv_bd.astype(BF16),
        "bf_vec": jnp.zeros((1, LANES), F32).at[0, LOGF_LANE:LOGF_LANE + FOX_HEADS].set(fox_bf[l]),
        "w_s": w_s[l], "bias": jnp.repeat(b_s[l].T, A_DIM, axis=1),
        "w0": jnp.repeat(w_s[l][:, 0, 0], A_DIM)[None, :], "b0": jnp.repeat(b_s[l][:, 0], A_DIM)[None, :],
        "w_gate": w_gate[l].astype(BF16), "w_br": br.astype(BF16), "w_o": w_o[l].astype(BF16),
        "g_mlp": g_mlp[l][None, :], "w_up": w_up[l].astype(BF16), "w_down": w_down[l].astype(BF16),
    }


def _rope_tables(pos):
    posf = pos.astype(F32)[:, None]
    lane = np.arange(LANES)

    def tables(half, period):
        inv = ROPE_THETA ** (-jnp.arange(half, dtype=F32) / half)
        ang = posf * inv[None, :]
        j = lane % period
        idx = j % half
        rot = j < 2 * half
        cos = jnp.where(rot[None, :], jnp.cos(ang)[:, idx], 1.0)
        sgn = np.where(j < half, -1.0, 1.0).astype(np.float32)
        sin = jnp.where(rot[None, :], jnp.sin(ang)[:, idx] * sgn[None, :], 0.0)
        return cos, sin

    cq, sq = tables(MLA_ROPE // 2, LANES)
    cm, sm = tables(ROT_DIM // 2, HEAD_DIM)
    return cq, sq, cm, sm


def _prompt_layer(x, lw, tabs, batch, seq, g_final, final):
    au, avn, fq, fk, fv, misc, ckvn, qc, mq, mk, mv = _front(x, lw, tabs, 256, seq // 256)
    ccol, crow, kmean = _prep(misc, mk, batch, seq)
    ya = _spatial(au, avn, lw["w_s"], lw["bias"], 512)
    yb = _fox_prompt(fq, fk, fv, ccol, crow, batch, seq, 256)
    ycl = _mla_prompt(qc, ckvn, misc, batch, seq, 256)
    yd = _moba_prompt(mq, mk, mv, kmean, batch, seq)
    x1 = _mix(x, ya, yb, ycl, yd, lw, 512)
    x2 = _mlp(x1, lw, g_final, final, 512)
    return x2, (fk, fv, misc, ckvn, mk, mv, avn)


def _unpack_rows(rows, lead):
    fk, fv, misc, ckvn, mk, mv, avn = rows
    return (fk.reshape(lead + (2, HEAD_DIM)), fv.reshape(lead + (2, HEAD_DIM)),
            misc[:, LOGF_LANE:LOGF_LANE + FOX_HEADS].reshape(lead + (FOX_HEADS,)),
            ckvn.reshape(lead + (MLA_KV_LORA,)), misc[:, :MLA_ROPE].reshape(lead + (MLA_ROPE,)),
            mk.reshape(lead + (2, HEAD_DIM)), mv.reshape(lead + (2, HEAD_DIM)),
            avn.reshape(lead + (4, A_DIM)))


def _prompt_trunk(x_prompt, lws, g_final):
    batch, seq, _ = x_prompt.shape
    tabs = _rope_tables(jnp.arange(seq))
    x = x_prompt.reshape(batch * seq, D_MODEL)
    rows = []
    for l, lw in enumerate(lws):
        x, r = _prompt_layer(x, lw, tabs, batch, seq, g_final[None, :], l == len(lws) - 1)
        rows.append(_unpack_rows(r, (batch, seq)))
    return x.reshape(batch, seq, D_MODEL), tuple(jnp.stack(r, axis=0) for r in zip(*rows))


PAGE_CHUNK = 32


def _scan_body(x_ref, w8_ref, tb8_ref):
    r = lax.broadcasted_iota(jnp.int32, (LANES, LANES), 0)
    c = lax.broadcasted_iota(jnp.int32, (LANES, LANES), 1)
    later = jnp.where(r > c, 1.0, 0.0).astype(BF16)
    ones = jnp.ones((LANES, LANES), BF16)
    eo = lax.broadcasted_iota(jnp.int32, (2 * LANES, LANES), 0)
    ei = lax.broadcasted_iota(jnp.int32, (2 * LANES, LANES), 1)
    expand = jnp.where(((eo >> 3) == (ei >> 2)) & ((eo & 7) == (ei & 3)), 1.0, 0.0).astype(BF16)
    for t in range(x_ref.shape[0] // LANES):
        x = x_ref[t * LANES:(t + 1) * LANES, :]
        w8_ref[2 * t * LANES:2 * (t + 1) * LANES, :] = _dot3_rhs(expand, _dot3_lhs(x, later))
        tb8_ref[2 * t * LANES:2 * (t + 1) * LANES, :] = _dot3_rhs(expand, _dot3_lhs(x, ones))


def _logf_scan(lf2):
    rows = lf2.shape[0]
    rb = 1024
    return pl.pallas_call(
        _scan_body,
        grid=(rows // rb,),
        in_specs=[pl.BlockSpec((rb, LANES), lambda i: (i, 0))],
        out_specs=[pl.BlockSpec((2 * rb, LANES), lambda i: (i, 0))] * 2,
        out_shape=[jax.ShapeDtypeStruct((2 * rows, LANES), F32)] * 2,
        compiler_params=_cparams("parallel"),
        name="logf_scan",
    )(lf2)


def _q8_pair(q_ref):
    qa, qb = q_ref[:, 0:128].astype(F32), q_ref[:, 128:256].astype(F32)
    row = lax.broadcasted_iota(jnp.int32, (8, LANES), 0)
    lo = lax.broadcasted_iota(jnp.int32, (8, LANES), 1) < HEAD_DIM
    zero = jnp.zeros((8, LANES), F32)
    q8 = jnp.where((row == 0) & lo, qa, jnp.where((row == 1) & lo, qb,
                   jnp.where((row == 2) & ~lo, qb, jnp.where((row == 3) & ~lo, qa, zero))))
    return q8.astype(BF16)


def _pair_out(o8, out_ref):
    lo = lax.broadcasted_iota(jnp.int32, (1, LANES), 1) < HEAD_DIM
    out_ref[:, 0:128] = jnp.where(lo, o8[0:1], o8[3:4]).astype(out_ref.dtype)
    out_ref[:, 128:256] = jnp.where(lo, o8[1:2], o8[2:3]).astype(out_ref.dtype)


def _chunk_softmax(s, vals, m_ref, l_ref, acc_ref, pv_fn):
    mx = s[0]
    for si in s[1:]:
        mx = jnp.maximum(mx, si)
    m_prev = m_ref[...]
    m_new = jnp.maximum(m_prev, jnp.max(mx, axis=-1, keepdims=True))
    alpha = jnp.exp(m_prev - m_new)
    psum = None
    pv = None
    for si, vi in zip(s, vals):
        p = jnp.exp(si - m_new)
        psum = p if psum is None else psum + p
        t = pv_fn(p.astype(BF16), vi)
        pv = t if pv is None else pv + t
    l_ref[...] = alpha * l_ref[...] + jnp.sum(psum, axis=-1, keepdims=True)
    acc_ref[...] = alpha * acc_ref[...] + pv
    m_ref[...] = m_new


def _paged_schedule(copies):
    b, j = pl.program_id(0), pl.program_id(1)
    nch = pl.num_programs(1)
    lin = b * nch + j
    slot = lin % 2

    @pl.when(lin == 0)
    def _():
        for d in copies(b, j, slot):
            d.start()

    @pl.when(lin + 1 < pl.num_programs(0) * nch)
    def _():
        nxt = lin + 1
        for d in copies(nxt // nch, nxt % nch, 1 - slot):
            d.start()

    for d in copies(b, j, slot):
        d.wait()
    return slot


def _fox_dec_body(pt_ref, fq_ref, fk_ref, fv_ref, misc_ref, kt_hbm, vt_hbm, w8_hbm, tb8_hbm, yb_ref,
                  kbuf, vbuf, wbuf, tbuf, sem, m_ref, l_ref, acc_ref, off_ref, *, layer, ch):
    j = pl.program_id(1)
    nch = pl.num_programs(1)

    def copies(bb, jj, sl):
        base = (nch - 1 - jj) * ch
        out = []
        for i in range(ch):
            page = pt_ref[bb, base + i]
            for hbm, buf in ((kt_hbm, kbuf), (vt_hbm, vbuf), (w8_hbm, wbuf), (tb8_hbm, tbuf)):
                out.append(pltpu.make_async_copy(hbm.at[layer, page], buf.at[sl, i], sem.at[sl]))
        return out

    slot = _paged_schedule(copies)
    q8 = _q8_pair(fq_ref)
    row = lax.broadcasted_iota(jnp.int32, (8, LANES), 0)

    @pl.when(j == 0)
    def _():
        m_ref[...] = jnp.sum(q8.astype(F32) * fk_ref[...], axis=-1, keepdims=True)
        l_ref[...] = jnp.ones(l_ref.shape, F32)
        acc_ref[...] = jnp.broadcast_to(fv_ref[...], acc_ref.shape)
        off = jnp.zeros((8, LANES), F32)
        for hd in range(FOX_HEADS):
            off = jnp.where(row == hd, misc_ref[:, LOGF_LANE + hd:LOGF_LANE + hd + 1], off)
        off_ref[...] = off

    run = off_ref[...]
    after = [None] * ch
    for i in reversed(range(ch)):
        after[i] = run
        run = run + tbuf[slot, i]
    off_ref[...] = run
    s = [_dot(q8, kbuf[slot, i].astype(BF16)) + (wbuf[slot, i] + after[i]) for i in range(ch)]
    vals = [vbuf[slot, i] for i in range(ch)]
    _chunk_softmax(s, vals, m_ref, l_ref, acc_ref, lambda p, v: _dot_nt(p, v.astype(BF16)))

    @pl.when(j == nch - 1)
    def _():
        _pair_out(acc_ref[...] / l_ref[...], yb_ref)


def _tok(w):
    return pl.BlockSpec((None, 1, w), lambda b, j, *_: (b, 0, 0))


_ANY = pl.BlockSpec(memory_space=pl.ANY)


def _fox_decode(pt, fq, fk, fv, misc, kt, vt, w8, tb8, layer):
    nseq, npages = pt.shape
    ch = PAGE_CHUNK
    gs = pltpu.PrefetchScalarGridSpec(
        num_scalar_prefetch=1,
        grid=(nseq, npages // ch),
        in_specs=[_tok(256), _tok(128), _tok(128), _tok(128), _ANY, _ANY, _ANY, _ANY],
        out_specs=_tok(256),
        scratch_shapes=[pltpu.VMEM((2, ch, LANES, LANES), F32), pltpu.VMEM((2, ch, LANES, LANES), F32),
                        pltpu.VMEM((2, ch, 8, LANES), F32), pltpu.VMEM((2, ch, 8, LANES), F32),
                        pltpu.SemaphoreType.DMA((2,)),
                        pltpu.VMEM((8, 1), F32), pltpu.VMEM((8, 1), F32), pltpu.VMEM((8, LANES), F32),
                        pltpu.VMEM((8, LANES), F32)],
    )
    return pl.pallas_call(
        functools.partial(_fox_dec_body, layer=layer, ch=ch),
        grid_spec=gs,
        out_shape=jax.ShapeDtypeStruct((nseq, 1, 256), BF16),
        compiler_params=_cparams("arbitrary", "arbitrary"),
        name="fox_decode",
    )(pt, fq, fk, fv, misc, kt, vt, w8, tb8)


def _mla_dec_body(pt_ref, qc_ref, ckvn_ref, misc_ref, ckv_hbm, krt_hbm, yc_ref,
                  cbuf, rbuf, sem, m_ref, l_ref, acc_ref, *, layer, ch):
    j = pl.program_id(1)
    nch = pl.num_programs(1)

    def copies(bb, jj, sl):
        out = []
        for i in range(ch):
            page = pt_ref[bb, jj * ch + i]
            out.append(pltpu.make_async_copy(ckv_hbm.at[layer, page], cbuf.at[sl, i], sem.at[sl]))
            out.append(pltpu.make_async_copy(krt_hbm.at[layer, page], rbuf.at[sl, i], sem.at[sl]))
        return out

    slot = _paged_schedule(copies)
    row = lax.broadcasted_iota(jnp.int32, (8, LANES), 0)
    zero = jnp.zeros((8, LANES), F32)
    qlf, qrf = zero, zero
    for hd in range(MLA_HEADS):
        qlf = jnp.where(row == hd, qc_ref[:, 256 * hd:256 * hd + 128].astype(F32), qlf)
        qrf = jnp.where(row == hd, qc_ref[:, 256 * hd + 128:256 * hd + 256].astype(F32), qrf)
    ql, qr = qlf.astype(BF16), qrf.astype(BF16)

    @pl.when(j == 0)
    def _():
        m_ref[...] = jnp.sum(qlf * ckvn_ref[...] + qrf * misc_ref[...], axis=-1, keepdims=True)
        l_ref[...] = jnp.ones(l_ref.shape, F32)
        acc_ref[...] = jnp.broadcast_to(ckvn_ref[...], acc_ref.shape)

    qr32 = qr[:, 0:MLA_ROPE]
    vals = [cbuf[slot, i].astype(BF16) for i in range(ch)]
    s = [_dot_nt(ql, vals[i]) + _dot(qr32, rbuf[slot, i].astype(BF16)) for i in range(ch)]
    _chunk_softmax(s, vals, m_ref, l_ref, acc_ref, _dot)

    @pl.when(j == nch - 1)
    def _():
        o8 = acc_ref[...] / l_ref[...]
        for hd in range(MLA_HEADS):
            yc_ref[:, 128 * hd:128 * hd + 128] = o8[hd:hd + 1].astype(BF16)


def _mla_decode(pt, qc, ckvn, misc, ckv, krt, layer):
    nseq, npages = pt.shape
    ch = PAGE_CHUNK
    gs = pltpu.PrefetchScalarGridSpec(
        num_scalar_prefetch=1,
        grid=(nseq, npages // ch),
        in_specs=[_tok(1024), _tok(128), _tok(128), _ANY, _ANY],
        out_specs=_tok(512),
        scratch_shapes=[pltpu.VMEM((2, ch, PAGE_SIZE, MLA_KV_LORA), F32), pltpu.VMEM((2, ch, MLA_ROPE, PAGE_SIZE), F32),
                        pltpu.SemaphoreType.DMA((2,)),
                        pltpu.VMEM((8, 1), F32), pltpu.VMEM((8, 1), F32), pltpu.VMEM((8, LANES), F32)],
    )
    return pl.pallas_call(
        functools.partial(_mla_dec_body, layer=layer, ch=ch),
        grid_spec=gs,
        out_shape=jax.ShapeDtypeStruct((nseq, 1, 512), BF16),
        compiler_params=_cparams("arbitrary", "arbitrary"),
        name="mla_decode",
    )(pt, qc, ckvn, misc, ckv, krt)


def _moba_gate_body(pt_ref, mq_ref, kt_hbm, idx_ref, kbuf, sem, g_ref, *, layer, ch):
    j = pl.program_id(1)
    nch = pl.num_programs(1)

    def copies(bb, jj, sl):
        return [pltpu.make_async_copy(kt_hbm.at[layer, pt_ref[bb, jj * ch + i]], kbuf.at[sl, i], sem.at[sl])
                for i in range(ch)]

    slot = _paged_schedule(copies)
    q8 = _q8_pair(mq_ref)
    lane = lax.broadcasted_iota(jnp.int32, (8, LANES), 1)
    ppb = MOBA_BLOCK // PAGE_SIZE
    nblk = ch // ppb

    @pl.when(j == 0)
    def _():
        g_ref[...] = jnp.zeros(g_ref.shape, F32)

    g = g_ref[...]
    for n in range(nblk):
        raw = None
        for u in range(ppb):
            t = _dot3_rhs(q8, kbuf[slot, n * ppb + u])
            raw = t if raw is None else raw + t
        g = jnp.where(lane == j * nblk + n, jnp.sum(raw, axis=-1, keepdims=True), g)
    g_ref[...] = g

    @pl.when(j == nch - 1)
    def _():
        ncand = nch * nblk
        gate = jnp.where(lane < ncand, g * (1.0 / (FOX_SCALE * MOBA_BLOCK)), NEG_INF)
        rank = _topk_rank(gate, ncand)
        lane_f = lane.astype(F32)
        out = jnp.zeros((8, LANES), F32)
        for t in range(MOBA_TOPK):
            pick = jnp.sum(jnp.where(rank == t, lane_f, 0.0), axis=-1, keepdims=True)
            out = jnp.where(lane == t, pick, out)
        idx_ref[...] = out.astype(jnp.int32)


def _moba_gate(pt, mq, kt, layer):
    nseq, npages = pt.shape
    ch = PAGE_CHUNK
    gs = pltpu.PrefetchScalarGridSpec(
        num_scalar_prefetch=1,
        grid=(nseq, npages // ch),
        in_specs=[_tok(256), _ANY],
        out_specs=pl.BlockSpec((None, 8, LANES), lambda b, j, *_: (b, 0, 0)),
        scratch_shapes=[pltpu.VMEM((2, ch, LANES, LANES), F32), pltpu.SemaphoreType.DMA((2,)),
                        pltpu.VMEM((8, LANES), F32)],
    )
    return pl.pallas_call(
        functools.partial(_moba_gate_body, layer=layer, ch=ch),
        grid_spec=gs,
        out_shape=jax.ShapeDtypeStruct((nseq, 8, LANES), jnp.int32),
        compiler_params=_cparams("arbitrary", "arbitrary"),
        name="moba_gate",
    )(pt, mq, kt)


def _moba_dec_body(pt_ref, idx_ref, mq_ref, mk_ref, mv_ref, kt_hbm, vt_hbm, yd_ref,
                   kbuf, vbuf, sem, m_ref, l_ref, acc_ref, *, layer):
    b = pl.program_id(0)
    ppb = MOBA_BLOCK // PAGE_SIZE
    heads = 4
    nsel = heads * MOBA_TOPK * ppb
    copies = []
    for hd in range(heads):
        for t in range(MOBA_TOPK):
            blk = idx_ref[(b * heads + hd) * MOBA_TOPK + t]
            for u in range(ppb):
                page = pt_ref[b, blk * ppb + u]
                at = (hd * MOBA_TOPK + t) * ppb + u
                copies.append(pltpu.make_async_copy(kt_hbm.at[layer, page], kbuf.at[at], sem.at[0]))
                copies.append(pltpu.make_async_copy(vt_hbm.at[layer, page], vbuf.at[at], sem.at[0]))
    for d in copies:
        d.start()
    q8 = _q8_pair(mq_ref)
    row = lax.broadcasted_iota(jnp.int32, (8, LANES), 0)
    m_ref[...] = jnp.sum(q8.astype(F32) * mk_ref[...], axis=-1, keepdims=True)
    l_ref[...] = jnp.ones(l_ref.shape, F32)
    acc_ref[...] = jnp.broadcast_to(mv_ref[...], acc_ref.shape)
    for d in copies:
        d.wait()
    s = [jnp.where(row == at // (MOBA_TOPK * ppb), _dot(q8, kbuf[at].astype(BF16)), NEG_INF) for at in range(nsel)]
    vals = [vbuf[at] for at in range(nsel)]
    _chunk_softmax(s, vals, m_ref, l_ref, acc_ref, lambda p, v: _dot_nt(p, v.astype(BF16)))
    _pair_out(acc_ref[...] / l_ref[...], yd_ref)


def _moba_decode(pt, idx, mq, mk, mv, kt, vt, layer):
    nseq = pt.shape[0]
    nsel = 4 * MOBA_TOPK * (MOBA_BLOCK // PAGE_SIZE)
    tok = lambda w: pl.BlockSpec((None, 1, w), lambda b, *_: (b, 0, 0))
    gs = pltpu.PrefetchScalarGridSpec(
        num_scalar_prefetch=2,
        grid=(nseq,),
        in_specs=[tok(256), tok(128), tok(128), _ANY, _ANY],
        out_specs=tok(256),
        scratch_shapes=[pltpu.VMEM((nsel, LANES, LANES), F32), pltpu.VMEM((nsel, LANES, LANES), F32),
                        pltpu.SemaphoreType.DMA((1,)),
                        pltpu.VMEM((8, 1), F32), pltpu.VMEM((8, 1), F32), pltpu.VMEM((8, LANES), F32)],
    )
    return pl.pallas_call(
        functools.partial(_moba_dec_body, layer=layer),
        grid_spec=gs,
        out_shape=jax.ShapeDtypeStruct((nseq, 1, 256), BF16),
        compiler_params=_cparams("arbitrary"),
        name="moba_decode",
    )(pt, idx, mq, mk, mv, kt, vt)


def _sample_layer(x, lw, tabs, pt, views, layer, g_final, final):
    kt_f, vt_f, w8, tb8, ckv, krt, kt_m, vt_m = views
    n = x.shape[0]
    au, avn, fq, fk, fv, misc, ckvn, qc, mq, mk, mv = _front(x, lw, tabs, n, 1)
    r3 = lambda a: a.reshape(n, 1, a.shape[-1])
    ya = _spatial1(au, avn, lw["w0"], lw["b0"])
    yb = _fox_decode(pt, r3(fq), r3(fk), r3(fv), r3(misc), kt_f, vt_f, w8, tb8, layer)
    ycl = _mla_decode(pt, r3(qc), r3(ckvn), r3(misc), ckv, krt, layer)
    idx = _moba_gate(pt, r3(mq), kt_m, layer)
    idx = idx[:, 0:4, 0:MOBA_TOPK].reshape(-1)
    yd = _moba_decode(pt, idx, r3(mq), r3(mk), r3(mv), kt_m, vt_m, layer)
    x1 = _mix(x, ya, yb.reshape(n, 256), ycl.reshape(n, 512), yd.reshape(n, 256), lw, n)
    x2 = _mlp(x1, lw, g_final, final, n)
    return x2, (fk, fv, misc, ckvn, mk, mv, avn)


def _page_views(cache_fox_k, cache_fox_v, cache_fox_logf, cache_mla_ckv, cache_mla_krope, cache_moba_k, cache_moba_v):
    depth, pool = cache_fox_k.shape[:2]
    kt = lambda c: jnp.transpose(c, (0, 1, 3, 4, 2)).reshape(depth, pool, 2 * HEAD_DIM, PAGE_SIZE)
    lf2 = jnp.transpose(cache_fox_logf, (0, 1, 3, 2)).reshape(depth * pool * FOX_HEADS, PAGE_SIZE)
    w8, tb8 = _logf_scan(lf2)
    w8 = w8.reshape(depth, pool, 8, PAGE_SIZE)
    tb8 = tb8.reshape(depth, pool, 8, PAGE_SIZE)
    krt = jnp.transpose(cache_mla_krope, (0, 1, 3, 2))
    return kt(cache_fox_k), kt(cache_fox_v), w8, tb8, cache_mla_ckv, krt, kt(cache_moba_k), kt(cache_moba_v)


def _sample_trunk(x_sample, lws, g_final, page_table, views):
    n = x_sample.shape[0]
    past_len = page_table.shape[1] * PAGE_SIZE
    tabs = _rope_tables(jnp.full((n,), past_len, jnp.int32))
    x = x_sample.reshape(n, D_MODEL)
    rows = []
    for l, lw in enumerate(lws):
        x, r = _sample_layer(x, lw, tabs, page_table, views, l, g_final[None, :], l == len(lws) - 1)
        rows.append(_unpack_rows(r, (n, 1)))
    return x.reshape(n, 1, D_MODEL), tuple(jnp.stack(r, axis=0) for r in zip(*rows))


def kernel(x_prompt, x_sample, cache_fox_k, cache_fox_v, cache_fox_logf, cache_mla_ckv, cache_mla_krope,
           cache_moba_k, cache_moba_v, page_table, g_attn, w_in, w_gate, g_av, w_s, b_s, fox_bf, g_cq, g_ckv,
           w_uq, w_uk, w_uv, w_br, w_o, g_mlp, w_up, w_down, g_final):
    assert x_sample.shape[1] == 1, "sample group is one new token per sequence"
    depth = w_in.shape[0]
    lws = [_layer_weights(l, g_attn, w_in, w_gate, g_av, w_s, b_s, fox_bf, g_cq, g_ckv, w_uq, w_uk, w_uv, w_br, w_o,
                          g_mlp, w_up, w_down) for l in range(depth)]
    y_prompt, p_rows = _prompt_trunk(x_prompt, lws, g_final)
    views = _page_views(cache_fox_k, cache_fox_v, cache_fox_logf, cache_mla_ckv, cache_mla_krope,
                        cache_moba_k, cache_moba_v)
    y_sample, s_rows = _sample_trunk(x_sample, lws, g_final, page_table, views)
    return (y_prompt, y_sample) + tuple(p_rows[:7]) + tuple(s_rows)
```

```python
import functools

import numpy as np
import jax
import jax.numpy as jnp
from jax import lax
from jax.experimental import pallas as pl
from jax.experimental.pallas import tpu as pltpu

F32 = jnp.float32
BF16 = jnp.bfloat16

D_MODEL = 1024
HEAD_DIM = 64
BR_WIDTH = 256
ROPE_THETA = 500000.0
NORM_EPS = 1e-6
NEG_INF = -1e30
D_FF = 4 * D_MODEL
CHUNK = 128
A_DIM = 64
FOX_HEADS = 4
MLA_HEADS = 4
MLA_NOPE = 64
MLA_ROPE = 32
MLA_KV_LORA = 128
MLA_Q_LORA = 256
MOBA_BLOCK = 256
MOBA_TOPK = 3
ROT_DIM = 16
PAGE_SIZE = 128

LANES = 128
D_INP = 16 * LANES
HEAD_ORDER = (0, 3, 1, 2)
FOX_SCALE = HEAD_DIM ** -0.5
MLA_SCALE = (MLA_NOPE + MLA_ROPE) ** -0.5
LOGF_LANE = 32
VMEM_LIMIT = 56 * 1024 * 1024


def _cparams(*sem):
    return pltpu.CompilerParams(dimension_semantics=sem, vmem_limit_bytes=VMEM_LIMIT)


def _rms(x, g):
    return x * lax.rsqrt(jnp.mean(x * x, axis=-1, keepdims=True) + NORM_EPS) * g


def _split3(a):
    a1 = a.astype(BF16)
    r1 = a - a1.astype(F32)
    a2 = r1.astype(BF16)
    a3 = (r1 - a2.astype(F32)).astype(BF16)
    return a1, a2, a3


def _dot(a, b):
    return jnp.dot(a, b, preferred_element_type=F32)


def _dot_nt(a, b):
    return lax.dot_general(a, b, (((1,), (1,)), ((), ())), preferred_element_type=F32)


def _dot3_lhs(a_f32, b_bf16):
    a1, a2, a3 = _split3(a_f32)
    return _dot(a1, b_bf16) + _dot(a2, b_bf16) + _dot(a3, b_bf16)


def _dot3_rhs(a_bf16, b_f32):
    b1, b2, b3 = _split3(b_f32)
    return _dot(a_bf16, b1) + _dot(a_bf16, b2) + _dot(a_bf16, b3)


def _rope(x, cos, sin, half, period):
    lane = lax.broadcasted_iota(jnp.int32, x.shape, 1) % period
    partner = jnp.where(lane < half, pltpu.roll(x, LANES - half, 1), pltpu.roll(x, half, 1))
    return x * cos + partner * sin


def _log_sigmoid(x):
    return jnp.minimum(x, 0.0) - jnp.log1p(jnp.exp(-jnp.abs(x)))


def _front_body(x_ref, ga_ref, win_ref, gav_ref, gmat_ref, gcq_ref, gckv_ref, wuq_ref, wuk_ref, bf_ref,
                cq_ref, sq_ref, cm_ref, sm_ref,
                au_ref, avn_ref, fq_ref, fk_ref, fv_ref, misc_ref, ckvn_ref, qc_ref, mq_ref, mk_ref, mv_ref,
                fvt_ref, kc_ref, ckvt_ref, mkb_ref, mvt_ref):
    x = x_ref[...]
    h = _rms(x, ga_ref[...]).astype(BF16)
    z = _dot(h, win_ref[...])
    au_ref[...] = z[:, 0:256]
    av = z[:, 256:512]
    ms = _dot3_lhs(av * av, gmat_ref[...]) * (1.0 / A_DIM)
    avn_ref[...] = av * lax.rsqrt(ms + NORM_EPS) * gav_ref[...]
    fq_ref[...] = (z[:, 512:768] * FOX_SCALE).astype(BF16)
    fk_ref[...] = z[:, 768:896]
    fv = z[:, 896:1024]
    fv_ref[...] = fv
    fvt_ref[...] = fv.T.astype(BF16)
    cos_q, sin_q = cq_ref[...], sq_ref[...]
    cqn = _rms(z[:, 1024:1280], gcq_ref[...]).astype(BF16)
    ckvn = _rms(z[:, 1280:1408], gckv_ref[...])
    ckvn_ref[...] = ckvn
    ckvt_ref[...] = ckvn.T.astype(BF16)
    kc_ref[:, 0:128] = ckvn.astype(BF16)
    qc = _dot(cqn, wuq_ref[...])
    qlat = _dot(qc[:, 0:256].astype(BF16), wuk_ref[...])
    for hd in range(MLA_HEADS):
        qc_ref[:, 256 * hd:256 * hd + 128] = (qlat[:, 128 * hd:128 * hd + 128] * MLA_SCALE).astype(BF16)
        qr = _rope(qc[:, 256 + 128 * hd:384 + 128 * hd], cos_q, sin_q, MLA_ROPE // 2, LANES)
        qc_ref[:, 256 * hd + 128:256 * hd + 256] = (qr * MLA_SCALE).astype(BF16)
    mb = z[:, 1408:1536]
    lane = lax.broadcasted_iota(jnp.int32, mb.shape, 1)
    kr = _rope(mb, cos_q, sin_q, MLA_ROPE // 2, LANES)
    logf = _log_sigmoid(mb + bf_ref[...])
    misc = jnp.where(lane < MLA_ROPE, kr, jnp.where(lane < LOGF_LANE + FOX_HEADS, logf, 0.0))
    misc_ref[...] = misc
    kc_ref[:, 128:256] = misc.astype(BF16)
    cos_m, sin_m = cm_ref[...], sm_ref[...]
    for blk in range(2):
        mq = _rope(z[:, 1536 + 128 * blk:1664 + 128 * blk], cos_m, sin_m, ROT_DIM // 2, HEAD_DIM)
        mq_ref[:, 128 * blk:128 * blk + 128] = (mq * FOX_SCALE).astype(BF16)
    mk = _rope(z[:, 1792:1920], cos_m, sin_m, ROT_DIM // 2, HEAD_DIM)
    mk_ref[...] = mk
    mkb_ref[...] = mk.astype(BF16)
    mv = z[:, 1920:2048]
    mv_ref[...] = mv
    mvt_ref[...] = mv.T.astype(BF16)


def _front(x, lw, tabs, tm, tab_blocks):
    m = x.shape[0]
    grid = (m // tm,)
    row = lambda w: pl.BlockSpec((tm, w), lambda i: (i, 0))
    full = lambda a: pl.BlockSpec(a.shape, lambda i: (0,) * a.ndim)
    tab = pl.BlockSpec((tm, LANES), lambda i: (i % tab_blocks, 0))
    widths = (256, 256, 256, 128, 128, 128, 128, 1024, 256, 128, 128)
    dtypes = (F32, F32, BF16, F32, F32, F32, F32, BF16, BF16, F32, F32)
    consts = (lw["g_attn"], lw["w_in"], lw["g_av"], lw["gmat"], lw["g_cq"], lw["g_ckv"], lw["w_uq"], lw["w_uk_bd"],
              lw["bf_vec"])
    tr = pl.BlockSpec((None, LANES, tm), lambda i: (i, 0, 0))
    tr_shape = jax.ShapeDtypeStruct((m // tm, LANES, tm), BF16)
    out_specs = [row(w) for w in widths] + [tr, row(256), tr, row(128), tr]
    out_shape = ([jax.ShapeDtypeStruct((m, w), d) for w, d in zip(widths, dtypes)]
                 + [tr_shape, jax.ShapeDtypeStruct((m, 256), BF16), tr_shape, jax.ShapeDtypeStruct((m, 128), BF16),
                    tr_shape])
    return pl.pallas_call(
        _front_body,
        grid=grid,
        in_specs=[row(D_MODEL)] + [full(a) for a in consts] + [tab] * 4,
        out_specs=out_specs,
        out_shape=out_shape,
        compiler_params=_cparams("parallel"),
        name="front",
    )(x, *consts, *tabs)


BIAS_K_LANE = 0
BIAS_Q_LANE = 16
N_SPLIT = 3


def _prep_body(misc_ref, fk_ref, mk_ref, k2_ref, cq_ref, kmean_ref):
    s = misc_ref.shape[0]
    r = lax.broadcasted_iota(jnp.int32, (LANES, LANES), 0)
    c = lax.broadcasted_iota(jnp.int32, (LANES, LANES), 1)
    tri = jnp.where(r >= c, 1.0, 0.0).astype(BF16)
    is_logf = (r >= LOGF_LANE) & (r < LOGF_LANE + FOX_HEADS)
    place_k = [jnp.where(is_logf & (c == BIAS_K_LANE + N_SPLIT * (r - LOGF_LANE) + t), 1.0, 0.0).astype(BF16)
               for t in range(N_SPLIT)]
    place_q = [jnp.where(is_logf & (c == BIAS_Q_LANE + N_SPLIT * (r - LOGF_LANE) + t), 1.0, 0.0).astype(BF16)
               for t in range(N_SPLIT)]
    lane = lax.broadcasted_iota(jnp.int32, (1, LANES), 1)
    n_terms = N_SPLIT * FOX_HEADS
    ones_k = jnp.where((lane >= BIAS_Q_LANE) & (lane < BIAS_Q_LANE + n_terms), 1.0, 0.0)
    ones_q = jnp.where((lane >= BIAS_K_LANE) & (lane < BIAS_K_LANE + n_terms), 1.0, 0.0)
    carry = jnp.zeros((1, LANES), F32)
    for t in range(s // LANES):
        sl = slice(t * LANES, (t + 1) * LANES)
        cs = _dot3_rhs(tri, misc_ref[sl, :]) + carry
        carry = cs[LANES - 1:LANES, :]
        parts = _split3(cs)
        ck = ones_k - (_dot(parts[0], place_k[0]) + _dot(parts[1], place_k[1]) + _dot(parts[2], place_k[2]))
        cq = ones_q + (_dot(parts[0], place_q[0]) + _dot(parts[1], place_q[1]) + _dot(parts[2], place_q[2]))
        k2_ref[sl, 0:128] = fk_ref[sl, :].astype(BF16)
        k2_ref[sl, 128:256] = ck.astype(BF16)
        cq_ref[sl, :] = cq.astype(BF16)
    nb = s // MOBA_BLOCK
    km = jnp.sum(mk_ref[...].reshape(nb, MOBA_BLOCK, LANES), axis=1) * (1.0 / MOBA_BLOCK)
    kmean_ref[...] = jnp.zeros(kmean_ref.shape, F32)
    kmean_ref[0:nb, :] = km


def _prep(misc, fk, mk, batch, seq):
    nrow = -(-(seq // MOBA_BLOCK) // 8) * 8
    return pl.pallas_call(
        _prep_body,
        grid=(batch,),
        in_specs=[pl.BlockSpec((seq, LANES), lambda b: (b, 0))] * 3,
        out_specs=[pl.BlockSpec((seq, 256), lambda b: (b, 0)),
                   pl.BlockSpec((seq, LANES), lambda b: (b, 0)),
                   pl.BlockSpec((None, nrow, LANES), lambda b: (b, 0, 0))],
        out_shape=[jax.ShapeDtypeStruct((batch * seq, 256), BF16),
                   jax.ShapeDtypeStruct((batch * seq, LANES), BF16),
                   jax.ShapeDtypeStruct((batch, nrow, LANES), F32)],
        compiler_params=_cparams("parallel"),
        name="prep",
    )(misc, fk, mk)


def _spatial_body(au_ref, avn_ref, ws_ref, bias_ref, ya_ref):
    rows = au_ref.shape[0]
    r = lax.broadcasted_iota(jnp.int32, (CHUNK, CHUNK), 0)
    c = lax.broadcasted_iota(jnp.int32, (CHUNK, CHUNK), 1)
    grp = lax.broadcasted_iota(jnp.int32, (CHUNK, BR_WIDTH), 1) // A_DIM
    ws = [jnp.where(r >= c, ws_ref[g], 0.0).astype(BF16) for g in range(4)]
    for t in range(rows // CHUNK):
        sl = slice(t * CHUNK, (t + 1) * CHUNK)
        v = avn_ref[sl, :].astype(BF16)
        s = bias_ref[...]
        for g in range(4):
            s = s + jnp.where(grp == g, _dot(ws[g], v), 0.0)
        ya_ref[sl, :] = (au_ref[sl, :] * s).astype(BF16)


def _spatial(au, avn, w_s, bias, tm):
    m = au.shape[0]
    row = pl.BlockSpec((tm, BR_WIDTH), lambda i: (i, 0))
    return pl.pallas_call(
        _spatial_body,
        grid=(m // tm,),
        in_specs=[row, row, pl.BlockSpec(w_s.shape, lambda i: (0, 0, 0)), pl.BlockSpec(bias.shape, lambda i: (0, 0))],
        out_specs=row,
        out_shape=jax.ShapeDtypeStruct((m, BR_WIDTH), BF16),
        compiler_params=_cparams("parallel"),
        name="spatial",
    )(au, avn, w_s, bias)


def _spatial1_body(au_ref, avn_ref, w0_ref, b0_ref, ya_ref):
    ya_ref[...] = (au_ref[...] * (w0_ref[...] * avn_ref[...] + b0_ref[...])).astype(BF16)


def _spatial1(au, avn, w0, b0):
    m = au.shape[0]
    return pl.pallas_call(
        _spatial1_body,
        out_shape=jax.ShapeDtypeStruct((m, BR_WIDTH), BF16),
        name="spatial1",
    )(au, avn, w0, b0)


ATT_TILE = 256


def _attn_tile(kt, vt, q_of, m_ref, l_ref, acc_ref, mask_fn):
    scores = [_dot_nt(kt, q_of(idx)) for idx in range(4)]
    probs = []
    alphas = []
    for idx in range(4):
        s = scores[idx] if mask_fn is None else mask_fn(idx, scores[idx])
        m_prev = m_ref[idx]
        m_new = jnp.maximum(m_prev, jnp.max(s, axis=0, keepdims=True))
        alpha = jnp.exp(m_prev - m_new)
        p = jnp.exp(s - m_new)
        l_ref[idx] = alpha * l_ref[idx] + jnp.sum(p, axis=0, keepdims=True)
        m_ref[idx] = m_new
        probs.append(p.astype(BF16))
        alphas.append(alpha)
    for idx in range(4):
        acc_ref[idx] = alphas[idx] * acc_ref[idx] + _dot(vt, probs[idx])


def _causal_t(t):
    return lax.broadcasted_iota(jnp.int32, (t, t), 0) <= lax.broadcasted_iota(jnp.int32, (t, t), 1)


def _past_tiles(i, k_ref, vt_ref, visit):
    def pair(jj, carry):
        j = 2 * jj
        kt = jnp.concatenate([k_ref[j], k_ref[j + 1]], axis=0)
        vt = jnp.concatenate([vt_ref[j], vt_ref[j + 1]], axis=1)
        visit(kt, vt, j, 2)
        return carry

    lax.fori_loop(0, i // 2, pair, 0)

    @pl.when(i % 2 == 1)
    def _():
        visit(k_ref[i - 1], vt_ref[i - 1], i - 1, 1)


def _init_stats(m_ref, l_ref, acc_ref):
    m_ref[...] = jnp.full(m_ref.shape, -jnp.inf, F32)
    l_ref[...] = jnp.zeros(l_ref.shape, F32)
    acc_ref[...] = jnp.zeros(acc_ref.shape, F32)


def _head_mask(qb, half):
    lane = lax.broadcasted_iota(jnp.int32, qb.shape, 1)
    keep = (lane < HEAD_DIM) if half == 0 else (lane >= HEAD_DIM)
    return jnp.where(keep, qb, jnp.zeros_like(qb))


def _merge_heads(acc_ref, l_ref, out_ref):
    row = lax.broadcasted_iota(jnp.int32, acc_ref.shape[1:], 0)
    for blk in range(2):
        o0 = acc_ref[2 * blk] / l_ref[2 * blk]
        o1 = acc_ref[2 * blk + 1] / l_ref[2 * blk + 1]
        out_ref[:, 128 * blk:128 * blk + 128] = jnp.where(row < HEAD_DIM, o0, o1).T.astype(out_ref.dtype)


def _attn_scratch(t, dk):
    return [pltpu.VMEM((4, t, dk), BF16), pltpu.VMEM((4, 1, t), F32), pltpu.VMEM((4, 1, t), F32),
            pltpu.VMEM((4, LANES, t), F32)]


def _fox_body(fq_ref, cq_ref, k2_ref, vt_ref, yb_ref, q_scr, m_ref, l_ref, acc_ref):
    i = pl.program_id(1)
    t = fq_ref.shape[0]
    _init_stats(m_ref, l_ref, acc_ref)
    lane = lax.broadcasted_iota(jnp.int32, (t, LANES), 1)
    cq = cq_ref[...]
    for blk in range(2):
        qb = fq_ref[:, 128 * blk:128 * blk + 128]
        for half in range(2):
            idx = 2 * blk + half
            lo_k = BIAS_K_LANE + N_SPLIT * HEAD_ORDER[idx]
            lo_q = BIAS_Q_LANE + N_SPLIT * HEAD_ORDER[idx]
            own = ((lane >= lo_k) & (lane < lo_k + N_SPLIT)) | ((lane >= lo_q) & (lane < lo_q + N_SPLIT))
            q_scr[idx, :, 0:128] = _head_mask(qb, half)
            q_scr[idx, :, 128:256] = jnp.where(own, cq, jnp.zeros_like(cq))
    q_of = lambda idx: q_scr[idx]
    _past_tiles(i, k2_ref, vt_ref, lambda kt, vt, j, n: _attn_tile(kt, vt, q_of, m_ref, l_ref, acc_ref, None))
    causal = _causal_t(t)
    _attn_tile(k2_ref[i], vt_ref[i], q_of, m_ref, l_ref, acc_ref, lambda idx, s: jnp.where(causal, s, NEG_INF))
    _merge_heads(acc_ref, l_ref, yb_ref)


def _seq_specs(n, t):
    qs = lambda w: pl.BlockSpec((t, w), lambda b, i: (b * n + i, 0))
    keys = lambda w: pl.BlockSpec((n, t, w), lambda b, i: (b, 0, 0))
    vals = pl.BlockSpec((n, LANES, t), lambda b, i: (b, 0, 0))
    return qs, keys, vals


def _fox_prompt(fq, cq, k2, vt, batch, seq):
    t = ATT_TILE
    n = seq // t
    qs, keys, vals = _seq_specs(n, t)
    return pl.pallas_call(
        _fox_body,
        grid=(batch, n),
        in_specs=[qs(256), qs(128), keys(256), vals],
        out_specs=qs(256),
        out_shape=jax.ShapeDtypeStruct((batch * seq, 256), BF16),
        scratch_shapes=_attn_scratch(t, 256),
        compiler_params=_cparams("parallel", "parallel"),
        name="fox_prompt",
    )(fq, cq, k2.reshape(batch * n, t, 256), vt)


def _mla_body(qc_ref, kc_ref, vt_ref, yc_ref, m_ref, l_ref, acc_ref):
    i = pl.program_id(1)
    t = qc_ref.shape[0]
    _init_stats(m_ref, l_ref, acc_ref)
    q_of = lambda hd: qc_ref[:, 256 * hd:256 * hd + 256]
    _past_tiles(i, kc_ref, vt_ref, lambda kt, vt, j, n: _attn_tile(kt, vt, q_of, m_ref, l_ref, acc_ref, None))
    causal = _causal_t(t)
    _attn_tile(kc_ref[i], vt_ref[i], q_of, m_ref, l_ref, acc_ref, lambda idx, s: jnp.where(causal, s, NEG_INF))
    for hd in range(MLA_HEADS):
        yc_ref[:, 128 * hd:128 * hd + 128] = (acc_ref[hd] / l_ref[hd]).T.astype(BF16)


def _mla_prompt(qc, kc, vt, batch, seq):
    t = ATT_TILE
    n = seq // t
    qs, keys, vals = _seq_specs(n, t)
    return pl.pallas_call(
        _mla_body,
        grid=(batch, n),
        in_specs=[qs(1024), keys(256), vals],
        out_specs=qs(512),
        out_shape=jax.ShapeDtypeStruct((batch * seq, 512), BF16),
        scratch_shapes=_attn_scratch(t, 256)[1:],
        compiler_params=_cparams("parallel", "parallel"),
        name="mla_prompt",
    )(qc, kc.reshape(batch * n, t, 256), vt)


def _topk_rank(g, n_cand):
    lane = lax.broadcasted_iota(jnp.int32, g.shape, 1)
    rank = jnp.zeros(g.shape, F32)
    for mth in range(n_cand):
        gm = g[:, mth:mth + 1]
        beats = (gm > g) | ((gm == g) & (mth < lane))
        rank = rank + jnp.where(beats, 1.0, 0.0)
    return rank


def _moba_body(mq_ref, k_ref, vt_ref, kmean_ref, yd_ref, q_scr, m_ref, l_ref, acc_ref, sel_ref):
    i = pl.program_id(1)
    t = mq_ref.shape[0]
    nrow = kmean_ref.shape[0]
    _init_stats(m_ref, l_ref, acc_ref)
    km = _split3(kmean_ref[...])
    blk_id = lax.broadcasted_iota(jnp.int32, (nrow, t), 0)
    for blk in range(2):
        qb = mq_ref[:, 128 * blk:128 * blk + 128]
        for half in range(2):
            idx = 2 * blk + half
            qm = _head_mask(qb, half)
            q_scr[idx] = qm
            gate = (_dot_nt(km[0], qm) + _dot_nt(km[1], qm) + _dot_nt(km[2], qm)) * (1.0 / FOX_SCALE)
            g = jnp.where(blk_id < i, gate, NEG_INF)
            rank = jnp.zeros((nrow, t), F32)
            for mth in range(nrow):
                gm = g[mth:mth + 1, :]
                beats = (gm > g) | ((gm == g) & (mth < blk_id))
                rank = rank + jnp.where(beats, 1.0, 0.0)
            sel_ref[idx] = jnp.where((rank < MOBA_TOPK) & (g > 0.5 * NEG_INF), 1.0, 0.0)
    q_of = lambda idx: q_scr[idx]
    causal = _causal_t(t)
    _attn_tile(k_ref[i], vt_ref[i], q_of, m_ref, l_ref, acc_ref, lambda idx, s: jnp.where(causal, s, NEG_INF))

    def visit(kt, vt, j, n):
        def picked(idx, s):
            sel = sel_ref[idx, pl.ds(j, 1), :]
            if n == 2:
                first = lax.broadcasted_iota(jnp.int32, s.shape, 0) < t
                sel = jnp.where(first, sel, sel_ref[idx, pl.ds(j + 1, 1), :])
            return jnp.where(sel > 0.0, s, NEG_INF)
        _attn_tile(kt, vt, q_of, m_ref, l_ref, acc_ref, picked)

    _past_tiles(i, k_ref, vt_ref, visit)
    _merge_heads(acc_ref, l_ref, yd_ref)


def _moba_prompt(mq, mkb, vt, kmean, batch, seq):
    t = MOBA_BLOCK
    n = seq // t
    nrow = kmean.shape[1]
    qs, keys, vals = _seq_specs(n, t)
    return pl.pallas_call(
        _moba_body,
        grid=(batch, n),
        in_specs=[qs(256), keys(128), vals, pl.BlockSpec((None, nrow, LANES), lambda b, i: (b, 0, 0))],
        out_specs=qs(256),
        out_shape=jax.ShapeDtypeStruct((batch * seq, 256), BF16),
        scratch_shapes=_attn_scratch(t, 128) + [pltpu.VMEM((4, nrow, t), F32)],
        compiler_params=_cparams("parallel", "parallel"),
        name="moba_prompt",
    )(mq, mkb.reshape(batch * n, t, 128), vt, kmean)


def _mix_body(x_ref, ga_ref, ya_ref, yb_ref, ycl_ref, yd_ref, wg_ref, wbr_ref, wuv_ref, wo_ref, x1_ref):
    x = x_ref[...]
    h = _rms(x, ga_ref[...]).astype(BF16)
    yc = _dot(ycl_ref[...], wuv_ref[...]).astype(BF16)
    ys = (ya_ref[...], yb_ref[...], yc, yd_ref[...])
    merged = None
    for b in range(4):
        term = jax.nn.sigmoid(_dot(h, wg_ref[b])) * _dot(ys[b], wbr_ref[b])
        merged = term if merged is None else merged + term
    x1_ref[...] = x + _dot(merged.astype(BF16), wo_ref[...])


def _mix(x, ya, yb, ycl, yd, lw, tm):
    m = x.shape[0]
    row = lambda w: pl.BlockSpec((tm, w), lambda i: (i, 0))
    full = lambda a: pl.BlockSpec(a.shape, lambda i: (0,) * a.ndim)
    consts = (lw["w_gate"], lw["w_br"], lw["w_uv_bd"], lw["w_o"])
    return pl.pallas_call(
        _mix_body,
        grid=(m // tm,),
        in_specs=[row(D_MODEL), full(lw["g_attn"]), row(256), row(256), row(512), row(256)] + [full(a) for a in consts],
        out_specs=row(D_MODEL),
        out_shape=jax.ShapeDtypeStruct((m, D_MODEL), F32),
        compiler_params=_cparams("parallel"),
        name="mix",
    )(x, lw["g_attn"], ya, yb, ycl, yd, *consts)


def _mlp_body(x_ref, g_ref, wup_ref, wdn_ref, gf_ref, o_ref, *, final):
    x = x_ref[...]
    h = _rms(x, g_ref[...]).astype(BF16)
    acc = x
    fc = 1024
    for c in range(D_FF // fc):
        u = jnp.maximum(_dot(h, wup_ref[:, c * fc:(c + 1) * fc]), 0.0)
        acc = acc + _dot((u * u).astype(BF16), wdn_ref[c * fc:(c + 1) * fc, :])
    o_ref[...] = _rms(acc, gf_ref[...]) if final else acc


def _mlp(x, lw, g_final, final, tm):
    m = x.shape[0]
    row = pl.BlockSpec((tm, D_MODEL), lambda i: (i, 0))
    full = lambda a: pl.BlockSpec(a.shape, lambda i: (0,) * a.ndim)
    consts = (lw["g_mlp"], lw["w_up"], lw["w_down"], g_final)
    return pl.pallas_call(
        functools.partial(_mlp_body, final=final),
        grid=(m // tm,),
        in_specs=[row] + [full(a) for a in consts],
        out_specs=row,
        out_shape=jax.ShapeDtypeStruct((m, D_MODEL), F32),
        compiler_params=_cparams("parallel"),
        name="mlp",
    )(x, *consts)


def _perm_heads(w, axis):
    parts = [lax.slice_in_dim(w, h * HEAD_DIM, (h + 1) * HEAD_DIM, axis=axis) for h in HEAD_ORDER]
    return jnp.concatenate(parts, axis=axis)


def _layer_weights(l, g_attn, w_in, w_gate, g_av, w_s, b_s, fox_bf, g_cq, g_ckv, w_uq, w_uk, w_uv, w_br, w_o,
                   g_mlp, w_up, w_down):
    wi = w_in[l]
    a_u, a_v, fq, fk, fv = wi[:, 0:256], wi[:, 256:512], wi[:, 512:768], wi[:, 768:896], wi[:, 896:1024]
    ff, cq, ckv, kr = wi[:, 1024:1028], wi[:, 1028:1284], wi[:, 1284:1412], wi[:, 1412:1444]
    mq, mk, mv = wi[:, 1444:1700], wi[:, 1700:1828], wi[:, 1828:1956]
    misc = jnp.concatenate([kr, ff, jnp.zeros((D_MODEL, LANES - MLA_ROPE - FOX_HEADS), F32)], axis=1)
    w_in_p = jnp.concatenate([a_u, a_v, _perm_heads(fq, 1), fk, fv, cq, ckv, misc, _perm_heads(mq, 1), mk, mv], axis=1)
    uq = w_uq[l]
    zpad = jnp.zeros((MLA_Q_LORA, LANES - MLA_ROPE), F32)
    uq_p = jnp.concatenate([uq[:, :, :MLA_NOPE].reshape(MLA_Q_LORA, MLA_HEADS * MLA_NOPE)]
                           + [a for h in range(MLA_HEADS) for a in (uq[:, h, MLA_NOPE:], zpad)], axis=1)
    uk_bd = jnp.zeros((MLA_HEADS * MLA_NOPE, MLA_HEADS * MLA_KV_LORA), F32)
    uv_bd = jnp.zeros((MLA_HEADS * MLA_KV_LORA, BR_WIDTH), F32)
    for h in range(MLA_HEADS):
        uk_bd = uk_bd.at[64 * h:64 * h + 64, 128 * h:128 * h + 128].set(w_uk[l][:, h, :].T)
        uv_bd = uv_bd.at[128 * h:128 * h + 128, 64 * h:64 * h + 64].set(w_uv[l][:, h, :])
    br = jnp.stack([w_br[l, 0], _perm_heads(w_br[l, 1], 0), w_br[l, 2], _perm_heads(w_br[l, 3], 0)], axis=0)
    grp = np.arange(BR_WIDTH) // A_DIM
    return {
        "g_attn": g_attn[l][None, :], "w_in": w_in_p.astype(BF16),
        "g_av": g_av[l].reshape(1, BR_WIDTH), "gmat": jnp.asarray(grp[:, None] == grp[None, :], BF16),
        "g_cq": g_cq[l][None, :], "g_ckv": g_ckv[l][None, :],
        "w_uq": uq_p.astype(BF16), "w_uk_bd": uk_bd.astype(BF16), "w_uv_bd": uv_bd.astype(BF16),
        "bf_vec": jnp.zeros((1, LANES), F32).at[0, LOGF_LANE:LOGF_LANE + FOX_HEADS].set(fox_bf[l]),
        "w_s": w_s[l], "bias": jnp.repeat(b_s[l].T, A_DIM, axis=1),
        "w0": jnp.repeat(w_s[l][:, 0, 0], A_DIM)[None, :], "b0": jnp.repeat(b_s[l][:, 0], A_DIM)[None, :],
        "w_gate": w_gate[l].astype(BF16), "w_br": br.astype(BF16), "w_o": w_o[l].astype(BF16),
        "g_mlp": g_mlp[l][None, :], "w_up": w_up[l].astype(BF16), "w_down": w_down[l].astype(BF16),
    }


def _rope_tables(pos):
    posf = pos.astype(F32)[:, None]
    lane = np.arange(LANES)

    def tables(half, period):
        inv = ROPE_THETA ** (-jnp.arange(half, dtype=F32) / half)
        ang = posf * inv[None, :]
        j = lane % period
        idx = j % half
        rot = j < 2 * half
        cos = jnp.where(rot[None, :], jnp.cos(ang)[:, idx], 1.0)
        sgn = np.where(j < half, -1.0, 1.0).astype(np.float32)
        sin = jnp.where(rot[None, :], jnp.sin(ang)[:, idx] * sgn[None, :], 0.0)
        return cos, sin

    cq, sq = tables(MLA_ROPE // 2, LANES)
    cm, sm = tables(ROT_DIM // 2, HEAD_DIM)
    return cq, sq, cm, sm


def _prompt_layer(x, lw, tabs, batch, seq, g_final, final):
    (au, avn, fq, fk, fv, misc, ckvn, qc, mq, mk, mv,
     fvt, kc, ckvt, mkb, mvt) = _front(x, lw, tabs, ATT_TILE, seq // ATT_TILE)
    k2, cq, kmean = _prep(misc, fk, mk, batch, seq)
    ya = _spatial(au, avn, lw["w_s"], lw["bias"], 512)
    yb = _fox_prompt(fq, cq, k2, fvt, batch, seq)
    ycl = _mla_prompt(qc, kc, ckvt, batch, seq)
    yd = _moba_prompt(mq, mkb, mvt, kmean, batch, seq)
    x1 = _mix(x, ya, yb, ycl, yd, lw, 512)
    x2 = _mlp(x1, lw, g_final, final, 512)
    return x2, (fk, fv, misc, ckvn, mk, mv, avn)


def _unpack_rows(rows, lead):
    fk, fv, misc, ckvn, mk, mv, avn = rows
    return (fk.reshape(lead + (2, HEAD_DIM)), fv.reshape(lead + (2, HEAD_DIM)),
            misc[:, LOGF_LANE:LOGF_LANE + FOX_HEADS].reshape(lead + (FOX_HEADS,)),
            ckvn.reshape(lead + (MLA_KV_LORA,)), misc[:, :MLA_ROPE].reshape(lead + (MLA_ROPE,)),
            mk.reshape(lead + (2, HEAD_DIM)), mv.reshape(lead + (2, HEAD_DIM)),
            avn.reshape(lead + (4, A_DIM)))


def _prompt_trunk(x_prompt, lws, g_final):
    batch, seq, _ = x_prompt.shape
    tabs = _rope_tables(jnp.arange(seq))
    x = x_prompt.reshape(batch * seq, D_MODEL)
    rows = []
    for l, lw in enumerate(lws):
        x, r = _prompt_layer(x, lw, tabs, batch, seq, g_final[None, :], l == len(lws) - 1)
        rows.append(_unpack_rows(r, (batch, seq)))
    return x.reshape(batch, seq, D_MODEL), tuple(jnp.stack(r, axis=0) for r in zip(*rows))


PAGE_CHUNK = 64


def _scan_body(x_ref, w8_ref, tb8_ref):
    r = lax.broadcasted_iota(jnp.int32, (LANES, LANES), 0)
    c = lax.broadcasted_iota(jnp.int32, (LANES, LANES), 1)
    later = jnp.where(r > c, 1.0, 0.0).astype(BF16)
    ones = jnp.ones((LANES, LANES), BF16)
    eo = lax.broadcasted_iota(jnp.int32, (2 * LANES, LANES), 0)
    ei = lax.broadcasted_iota(jnp.int32, (2 * LANES, LANES), 1)
    expand = jnp.where(((eo >> 3) == (ei >> 2)) & ((eo & 7) == (ei & 3)), 1.0, 0.0).astype(BF16)
    for t in range(x_ref.shape[0] // LANES):
        x = x_ref[t * LANES:(t + 1) * LANES, :]
        w8_ref[2 * t * LANES:2 * (t + 1) * LANES, :] = _dot3_rhs(expand, _dot3_lhs(x, later))
        tb8_ref[2 * t * LANES:2 * (t + 1) * LANES, :] = _dot3_rhs(expand, _dot3_lhs(x, ones))


def _logf_scan(lf2):
    rows = lf2.shape[0]
    rb = 1024
    return pl.pallas_call(
        _scan_body,
        grid=(rows // rb,),
        in_specs=[pl.BlockSpec((rb, LANES), lambda i: (i, 0))],
        out_specs=[pl.BlockSpec((2 * rb, LANES), lambda i: (i, 0))] * 2,
        out_shape=[jax.ShapeDtypeStruct((2 * rows, LANES), F32)] * 2,
        compiler_params=_cparams("parallel"),
        name="logf_scan",
    )(lf2)


def _q8_pair(q_ref):
    qa, qb = q_ref[:, 0:128].astype(F32), q_ref[:, 128:256].astype(F32)
    row = lax.broadcasted_iota(jnp.int32, (8, LANES), 0)
    lo = lax.broadcasted_iota(jnp.int32, (8, LANES), 1) < HEAD_DIM
    zero = jnp.zeros((8, LANES), F32)
    q8 = jnp.where((row == 0) & lo, qa, jnp.where((row == 1) & lo, qb,
                   jnp.where((row == 2) & ~lo, qb, jnp.where((row == 3) & ~lo, qa, zero))))
    return q8.astype(BF16)


def _pair_out(o8, out_ref):
    lo = lax.broadcasted_iota(jnp.int32, (1, LANES), 1) < HEAD_DIM
    out_ref[:, 0:128] = jnp.where(lo, o8[0:1], o8[3:4]).astype(out_ref.dtype)
    out_ref[:, 128:256] = jnp.where(lo, o8[1:2], o8[2:3]).astype(out_ref.dtype)


def _chunk_softmax(s, vals, m_ref, l_ref, acc_ref, pv_fn):
    mx = s[0]
    for si in s[1:]:
        mx = jnp.maximum(mx, si)
    m_prev = m_ref[...]
    m_new = jnp.maximum(m_prev, jnp.max(mx, axis=-1, keepdims=True))
    alpha = jnp.exp(m_prev - m_new)
    psum = None
    pv = None
    for si, vi in zip(s, vals):
        p = jnp.exp(si - m_new)
        psum = p if psum is None else psum + p
        t = pv_fn(p.astype(BF16), vi)
        pv = t if pv is None else pv + t
    l_ref[...] = alpha * l_ref[...] + jnp.sum(psum, axis=-1, keepdims=True)
    acc_ref[...] = alpha * acc_ref[...] + pv
    m_ref[...] = m_new


def _paged_schedule(copies):
    b, j = pl.program_id(0), pl.program_id(1)
    nch = pl.num_programs(1)
    lin = b * nch + j
    slot = lin % 2

    @pl.when(lin == 0)
    def _():
        for d in copies(b, j, slot):
            d.start()

    @pl.when(lin + 1 < pl.num_programs(0) * nch)
    def _():
        nxt = lin + 1
        for d in copies(nxt // nch, nxt % nch, 1 - slot):
            d.start()

    for d in copies(b, j, slot):
        d.wait()
    return slot


def _fox_dec_body(pt_ref, fq_ref, fk_ref, fv_ref, misc_ref, kt_hbm, vt_hbm, w8_hbm, tb8_hbm, yb_ref,
                  kbuf, vbuf, wbuf, tbuf, sem, m_ref, l_ref, acc_ref, off_ref, *, layer, ch):
    j = pl.program_id(1)
    nch = pl.num_programs(1)

    def copies(bb, jj, sl):
        base = (nch - 1 - jj) * ch
        out = []
        for i in range(ch):
            page = pt_ref[bb, base + i]
            for hbm, buf in ((kt_hbm, kbuf), (vt_hbm, vbuf), (w8_hbm, wbuf), (tb8_hbm, tbuf)):
                out.append(pltpu.make_async_copy(hbm.at[layer, page], buf.at[sl, i], sem.at[sl]))
        return out

    slot = _paged_schedule(copies)
    q8 = _q8_pair(fq_ref)
    row = lax.broadcasted_iota(jnp.int32, (8, LANES), 0)

    @pl.when(j == 0)
    def _():
        m_ref[...] = jnp.sum(q8.astype(F32) * fk_ref[...], axis=-1, keepdims=True)
        l_ref[...] = jnp.ones(l_ref.shape, F32)
        acc_ref[...] = jnp.broadcast_to(fv_ref[...], acc_ref.shape)
        off = jnp.zeros((8, LANES), F32)
        for hd in range(FOX_HEADS):
            off = jnp.where(row == hd, misc_ref[:, LOGF_LANE + hd:LOGF_LANE + hd + 1], off)
        off_ref[...] = off

    run = off_ref[...]
    after = [None] * ch
    for i in reversed(range(ch)):
        after[i] = run
        run = run + tbuf[slot, i]
    off_ref[...] = run
    s = [_dot(q8, kbuf[slot, i].astype(BF16)) + (wbuf[slot, i] + after[i]) for i in range(ch)]
    vals = [vbuf[slot, i] for i in range(ch)]
    _chunk_softmax(s, vals, m_ref, l_ref, acc_ref, lambda p, v: _dot_nt(p, v.astype(BF16)))

    @pl.when(j == nch - 1)
    def _():
        _pair_out(acc_ref[...] / l_ref[...], yb_ref)


def _tok(w):
    return pl.BlockSpec((None, 1, w), lambda b, j, *_: (b, 0, 0))


_ANY = pl.BlockSpec(memory_space=pl.ANY)


def _fox_decode(pt, fq, fk, fv, misc, kt, vt, w8, tb8, layer):
    nseq, npages = pt.shape
    ch = PAGE_CHUNK
    gs = pltpu.PrefetchScalarGridSpec(
        num_scalar_prefetch=1,
        grid=(nseq, npages // ch),
        in_specs=[_tok(256), _tok(128), _tok(128), _tok(128), _ANY, _ANY, _ANY, _ANY],
        out_specs=_tok(256),
        scratch_shapes=[pltpu.VMEM((2, ch, LANES, LANES), F32), pltpu.VMEM((2, ch, LANES, LANES), F32),
                        pltpu.VMEM((2, ch, 8, LANES), F32), pltpu.VMEM((2, ch, 8, LANES), F32),
                        pltpu.SemaphoreType.DMA((2,)),
                        pltpu.VMEM((8, 1), F32), pltpu.VMEM((8, 1), F32), pltpu.VMEM((8, LANES), F32),
                        pltpu.VMEM((8, LANES), F32)],
    )
    return pl.pallas_call(
        functools.partial(_fox_dec_body, layer=layer, ch=ch),
        grid_spec=gs,
        out_shape=jax.ShapeDtypeStruct((nseq, 1, 256), BF16),
        compiler_params=_cparams("arbitrary", "arbitrary"),
        name="fox_decode",
    )(pt, fq, fk, fv, misc, kt, vt, w8, tb8)


def _mla_dec_body(pt_ref, qc_ref, ckvn_ref, misc_ref, ckv_hbm, krt_hbm, yc_ref,
                  cbuf, rbuf, sem, m_ref, l_ref, acc_ref, *, layer, ch):
    j = pl.program_id(1)
    nch = pl.num_programs(1)

    def copies(bb, jj, sl):
        out = []
        for i in range(ch):
            page = pt_ref[bb, jj * ch + i]
            out.append(pltpu.make_async_copy(ckv_hbm.at[layer, page], cbuf.at[sl, i], sem.at[sl]))
            out.append(pltpu.make_async_copy(krt_hbm.at[layer, page], rbuf.at[sl, i], sem.at[sl]))
        return out

    slot = _paged_schedule(copies)
    row = lax.broadcasted_iota(jnp.int32, (8, LANES), 0)
    zero = jnp.zeros((8, LANES), F32)
    qlf, qrf = zero, zero
    for hd in range(MLA_HEADS):
        qlf = jnp.where(row == hd, qc_ref[:, 256 * hd:256 * hd + 128].astype(F32), qlf)
        qrf = jnp.where(row == hd, qc_ref[:, 256 * hd + 128:256 * hd + 256].astype(F32), qrf)
    ql, qr = qlf.astype(BF16), qrf.astype(BF16)

    @pl.when(j == 0)
    def _():
        m_ref[...] = jnp.sum(qlf * ckvn_ref[...] + qrf * misc_ref[...], axis=-1, keepdims=True)
        l_ref[...] = jnp.ones(l_ref.shape, F32)
        acc_ref[...] = jnp.broadcast_to(ckvn_ref[...], acc_ref.shape)

    qr32 = qr[:, 0:MLA_ROPE]
    vals = [cbuf[slot, i].astype(BF16) for i in range(ch)]
    s = [_dot_nt(ql, vals[i]) + _dot(qr32, rbuf[slot, i].astype(BF16)) for i in range(ch)]
    _chunk_softmax(s, vals, m_ref, l_ref, acc_ref, _dot)

    @pl.when(j == nch - 1)
    def _():
        o8 = acc_ref[...] / l_ref[...]
        for hd in range(MLA_HEADS):
            yc_ref[:, 128 * hd:128 * hd + 128] = o8[hd:hd + 1].astype(BF16)


def _mla_decode(pt, qc, ckvn, misc, ckv, krt, layer):
    nseq, npages = pt.shape
    ch = PAGE_CHUNK
    gs = pltpu.PrefetchScalarGridSpec(
        num_scalar_prefetch=1,
        grid=(nseq, npages // ch),
        in_specs=[_tok(1024), _tok(128), _tok(128), _ANY, _ANY],
        out_specs=_tok(512),
        scratch_shapes=[pltpu.VMEM((2, ch, PAGE_SIZE, MLA_KV_LORA), F32), pltpu.VMEM((2, ch, MLA_ROPE, PAGE_SIZE), F32),
                        pltpu.SemaphoreType.DMA((2,)),
                        pltpu.VMEM((8, 1), F32), pltpu.VMEM((8, 1), F32), pltpu.VMEM((8, LANES), F32)],
    )
    return pl.pallas_call(
        functools.partial(_mla_dec_body, layer=layer, ch=ch),
        grid_spec=gs,
        out_shape=jax.ShapeDtypeStruct((nseq, 1, 512), BF16),
        compiler_params=_cparams("arbitrary", "arbitrary"),
        name="mla_decode",
    )(pt, qc, ckvn, misc, ckv, krt)


def _moba_gate_body(pt_ref, mq_ref, kt_hbm, idx_ref, kbuf, sem, g_ref, *, layer, ch):
    j = pl.program_id(1)
    nch = pl.num_programs(1)

    def copies(bb, jj, sl):
        return [pltpu.make_async_copy(kt_hbm.at[layer, pt_ref[bb, jj * ch + i]], kbuf.at[sl, i], sem.at[sl])
                for i in range(ch)]

    slot = _paged_schedule(copies)
    q8 = _q8_pair(mq_ref)
    lane = lax.broadcasted_iota(jnp.int32, (8, LANES), 1)
    ppb = MOBA_BLOCK // PAGE_SIZE
    nblk = ch // ppb

    @pl.when(j == 0)
    def _():
        g_ref[...] = jnp.zeros(g_ref.shape, F32)

    g = g_ref[...]
    for n in range(nblk):
        raw = None
        for u in range(ppb):
            t = _dot3_rhs(q8, kbuf[slot, n * ppb + u])
            raw = t if raw is None else raw + t
        g = jnp.where(lane == j * nblk + n, jnp.sum(raw, axis=-1, keepdims=True), g)
    g_ref[...] = g

    @pl.when(j == nch - 1)
    def _():
        ncand = nch * nblk
        gate = jnp.where(lane < ncand, g * (1.0 / (FOX_SCALE * MOBA_BLOCK)), NEG_INF)
        rank = _topk_rank(gate, ncand)
        lane_f = lane.astype(F32)
        out = jnp.zeros((8, LANES), F32)
        for t in range(MOBA_TOPK):
            pick = jnp.sum(jnp.where(rank == t, lane_f, 0.0), axis=-1, keepdims=True)
            out = jnp.where(lane == t, pick, out)
        idx_ref[...] = out.astype(jnp.int32)


def _moba_gate(pt, mq, kt, layer):
    nseq, npages = pt.shape
    ch = PAGE_CHUNK
    gs = pltpu.PrefetchScalarGridSpec(
        num_scalar_prefetch=1,
        grid=(nseq, npages // ch),
        in_specs=[_tok(256), _ANY],
        out_specs=pl.BlockSpec((None, 8, LANES), lambda b, j, *_: (b, 0, 0)),
        scratch_shapes=[pltpu.VMEM((2, ch, LANES, LANES), F32), pltpu.SemaphoreType.DMA((2,)),
                        pltpu.VMEM((8, LANES), F32)],
    )
    return pl.pallas_call(
        functools.partial(_moba_gate_body, layer=layer, ch=ch),
        grid_spec=gs,
        out_shape=jax.ShapeDtypeStruct((nseq, 8, LANES), jnp.int32),
        compiler_params=_cparams("arbitrary", "arbitrary"),
        name="moba_gate",
    )(pt, mq, kt)


def _moba_dec_body(pt_ref, idx_ref, mq_ref, mk_ref, mv_ref, kt_hbm, vt_hbm, yd_ref,
                   kbuf, vbuf, sem, m_ref, l_ref, acc_ref, *, layer):
    ppb = MOBA_BLOCK // PAGE_SIZE
    heads = 4
    nsel = heads * MOBA_TOPK * ppb

    def copies(bb, jj, sl):
        out = []
        for hd in range(heads):
            for t in range(MOBA_TOPK):
                blk = idx_ref[(bb * heads + hd) * MOBA_TOPK + t]
                for u in range(ppb):
                    page = pt_ref[bb, blk * ppb + u]
                    at = (hd * MOBA_TOPK + t) * ppb + u
                    out.append(pltpu.make_async_copy(kt_hbm.at[layer, page], kbuf.at[sl, at], sem.at[sl]))
                    out.append(pltpu.make_async_copy(vt_hbm.at[layer, page], vbuf.at[sl, at], sem.at[sl]))
        return out

    slot = _paged_schedule(copies)
    q8 = _q8_pair(mq_ref)
    row = lax.broadcasted_iota(jnp.int32, (8, LANES), 0)
    m_ref[...] = jnp.sum(q8.astype(F32) * mk_ref[...], axis=-1, keepdims=True)
    l_ref[...] = jnp.ones(l_ref.shape, F32)
    acc_ref[...] = jnp.broadcast_to(mv_ref[...], acc_ref.shape)
    s = [jnp.where(row == at // (MOBA_TOPK * ppb), _dot(q8, kbuf[slot, at].astype(BF16)), NEG_INF)
         for at in range(nsel)]
    vals = [vbuf[slot, at] for at in range(nsel)]
    _chunk_softmax(s, vals, m_ref, l_ref, acc_ref, lambda p, v: _dot_nt(p, v.astype(BF16)))
    _pair_out(acc_ref[...] / l_ref[...], yd_ref)


def _moba_decode(pt, idx, mq, mk, mv, kt, vt, layer):
    nseq = pt.shape[0]
    nsel = 4 * MOBA_TOPK * (MOBA_BLOCK // PAGE_SIZE)
    gs = pltpu.PrefetchScalarGridSpec(
        num_scalar_prefetch=2,
        grid=(nseq, 1),
        in_specs=[_tok(256), _tok(128), _tok(128), _ANY, _ANY],
        out_specs=_tok(256),
        scratch_shapes=[pltpu.VMEM((2, nsel, LANES, LANES), F32), pltpu.VMEM((2, nsel, LANES, LANES), F32),
                        pltpu.SemaphoreType.DMA((2,)),
                        pltpu.VMEM((8, 1), F32), pltpu.VMEM((8, 1), F32), pltpu.VMEM((8, LANES), F32)],
    )
    return pl.pallas_call(
        functools.partial(_moba_dec_body, layer=layer),
        grid_spec=gs,
        out_shape=jax.ShapeDtypeStruct((nseq, 1, 256), BF16),
        compiler_params=_cparams("arbitrary", "arbitrary"),
        name="moba_decode",
    )(pt, idx, mq, mk, mv, kt, vt)


def _sample_layer(x, lw, tabs, pt, views, layer, g_final, final):
    kt_f, vt_f, w8, tb8, ckv, krt, kt_m, vt_m = views
    n = x.shape[0]
    au, avn, fq, fk, fv, misc, ckvn, qc, mq, mk, mv = _front(x, lw, tabs, n, 1)[:11]
    r3 = lambda a: a.reshape(n, 1, a.shape[-1])
    ya = _spatial1(au, avn, lw["w0"], lw["b0"])
    yb = _fox_decode(pt, r3(fq), r3(fk), r3(fv), r3(misc), kt_f, vt_f, w8, tb8, layer)
    ycl = _mla_decode(pt, r3(qc), r3(ckvn), r3(misc), ckv, krt, layer)
    idx = _moba_gate(pt, r3(mq), kt_m, layer)
    idx = idx[:, 0:4, 0:MOBA_TOPK].reshape(-1)
    yd = _moba_decode(pt, idx, r3(mq), r3(mk), r3(mv), kt_m, vt_m, layer)
    x1 = _mix(x, ya, yb.reshape(n, 256), ycl.reshape(n, 512), yd.reshape(n, 256), lw, n)
    x2 = _mlp(x1, lw, g_final, final, n)
    return x2, (fk, fv, misc, ckvn, mk, mv, avn)


def _page_views(cache_fox_k, cache_fox_v, cache_fox_logf, cache_mla_ckv, cache_mla_krope, cache_moba_k, cache_moba_v):
    depth, pool = cache_fox_k.shape[:2]
    kt = lambda c: jnp.transpose(c, (0, 1, 3, 4, 2)).reshape(depth, pool, 2 * HEAD_DIM, PAGE_SIZE)
    lf2 = jnp.transpose(cache_fox_logf, (0, 1, 3, 2)).reshape(depth * pool * FOX_HEADS, PAGE_SIZE)
    w8, tb8 = _logf_scan(lf2)
    w8 = w8.reshape(depth, pool, 8, PAGE_SIZE)
    tb8 = tb8.reshape(depth, pool, 8, PAGE_SIZE)
    krt = jnp.transpose(cache_mla_krope, (0, 1, 3, 2))
    return kt(cache_fox_k), kt(cache_fox_v), w8, tb8, cache_mla_ckv, krt, kt(cache_moba_k), kt(cache_moba_v)


def _sample_trunk(x_sample, lws, g_final, page_table, views):
    n = x_sample.shape[0]
    past_len = page_table.shape[1] * PAGE_SIZE
    tabs = _rope_tables(jnp.full((n,), past_len, jnp.int32))
    x = x_sample.reshape(n, D_MODEL)
    rows = []
    for l, lw in enumerate(lws):
        x, r = _sample_layer(x, lw, tabs, page_table, views, l, g_final[None, :], l == len(lws) - 1)
        rows.append(_unpack_rows(r, (n, 1)))
    return x.reshape(n, 1, D_MODEL), tuple(jnp.stack(r, axis=0) for r in zip(*rows))


def kernel(x_prompt, x_sample, cache_fox_k, cache_fox_v, cache_fox_logf, cache_mla_ckv, cache_mla_krope,
           cache_moba_k, cache_moba_v, page_table, g_attn, w_in, w_gate, g_av, w_s, b_s, fox_bf, g_cq, g_ckv,
           w_uq, w_uk, w_uv, w_br, w_o, g_mlp, w_up, w_down, g_final):
    assert x_sample.shape[1] == 1, "sample group is one new token per sequence"
    depth = w_in.shape[0]
    lws = [_layer_weights(l, g_attn, w_in, w_gate, g_av, w_s, b_s, fox_bf, g_cq, g_ckv, w_uq, w_uk, w_uv, w_br, w_o,
                          g_mlp, w_up, w_down) for l in range(depth)]
    y_prompt, p_rows = _prompt_trunk(x_prompt, lws, g_final)
    views = _page_views(cache_fox_k, cache_fox_v, cache_fox_logf, cache_mla_ckv, cache_mla_krope,
                        cache_moba_k, cache_moba_v)
    y_sample, s_rows = _sample_trunk(x_sample, lws, g_final, page_table, views)
    return (y_prompt, y_sample) + tuple(p_rows[:7]) + tuple(s_rows)
```

```python
import functools

import numpy as np
import jax
import jax.numpy as jnp
from jax import lax
from jax.experimental import pallas as pl
from jax.experimental.pallas import tpu as pltpu

F32 = jnp.float32
BF16 = jnp.bfloat16

D_MODEL = 1024
HEAD_DIM = 64
BR_WIDTH = 256
ROPE_THETA = 500000.0
NORM_EPS = 1e-6
NEG_INF = -1e30
D_FF = 4 * D_MODEL
CHUNK = 128
A_DIM = 64
FOX_HEADS = 4
MLA_HEADS = 4
MLA_NOPE = 64
MLA_ROPE = 32
MLA_KV_LORA = 128
MLA_Q_LORA = 256
MOBA_BLOCK = 256
MOBA_TOPK = 3
ROT_DIM = 16
PAGE_SIZE = 128

LANES = 128
D_INP = 16 * LANES
HEAD_ORDER = (0, 3, 1, 2)
FOX_SCALE = HEAD_DIM ** -0.5
MLA_SCALE = (MLA_NOPE + MLA_ROPE) ** -0.5
LOGF_LANE = 32
VMEM_LIMIT = 56 * 1024 * 1024


def _cparams(*sem):
    return pltpu.CompilerParams(dimension_semantics=sem, vmem_limit_bytes=VMEM_LIMIT)


def _rms(x, g):
    return x * lax.rsqrt(jnp.mean(x * x, axis=-1, keepdims=True) + NORM_EPS) * g


def _split3(a):
    a1 = a.astype(BF16)
    r1 = a - a1.astype(F32)
    a2 = r1.astype(BF16)
    a3 = (r1 - a2.astype(F32)).astype(BF16)
    return a1, a2, a3


def _dot(a, b):
    return jnp.dot(a, b, preferred_element_type=F32)


def _dot_nt(a, b):
    return lax.dot_general(a, b, (((1,), (1,)), ((), ())), preferred_element_type=F32)


def _dot3_lhs(a_f32, b_bf16):
    a1, a2, a3 = _split3(a_f32)
    return _dot(a1, b_bf16) + _dot(a2, b_bf16) + _dot(a3, b_bf16)


def _dot3_rhs(a_bf16, b_f32):
    b1, b2, b3 = _split3(b_f32)
    return _dot(a_bf16, b1) + _dot(a_bf16, b2) + _dot(a_bf16, b3)


def _rope(x, cos, sin, half, period):
    lane = lax.broadcasted_iota(jnp.int32, x.shape, 1) % period
    partner = jnp.where(lane < half, pltpu.roll(x, LANES - half, 1), pltpu.roll(x, half, 1))
    return x * cos + partner * sin


def _log_sigmoid(x):
    return jnp.minimum(x, 0.0) - jnp.log1p(jnp.exp(-jnp.abs(x)))


def _front_body(x_ref, ga_ref, win_ref, gav_ref, gmat_ref, gcq_ref, gckv_ref, wuq_ref, wuk_ref, bf_ref,
                cq_ref, sq_ref, cm_ref, sm_ref,
                au_ref, avn_ref, fq_ref, fk_ref, fv_ref, misc_ref, ckvn_ref, qc_ref, mq_ref, mk_ref, mv_ref,
                fvt_ref, kc_ref, ckvt_ref, mkb_ref, mvt_ref):
    x = x_ref[...]
    h = _rms(x, ga_ref[...]).astype(BF16)
    z = _dot(h, win_ref[...])
    au_ref[...] = z[:, 0:256]
    av = z[:, 256:512]
    ms = _dot3_lhs(av * av, gmat_ref[...]) * (1.0 / A_DIM)
    avn_ref[...] = av * lax.rsqrt(ms + NORM_EPS) * gav_ref[...]
    fq_ref[...] = (z[:, 512:768] * FOX_SCALE).astype(BF16)
    fk_ref[...] = z[:, 768:896]
    fv = z[:, 896:1024]
    fv_ref[...] = fv
    fvt_ref[...] = fv.T.astype(BF16)
    cos_q, sin_q = cq_ref[...], sq_ref[...]
    cqn = _rms(z[:, 1024:1280], gcq_ref[...]).astype(BF16)
    ckvn = _rms(z[:, 1280:1408], gckv_ref[...])
    ckvn_ref[...] = ckvn
    ckvt_ref[...] = ckvn.T.astype(BF16)
    kc_ref[:, 0:128] = ckvn.astype(BF16)
    qc = _dot(cqn, wuq_ref[...])
    qlat = _dot(qc[:, 0:256].astype(BF16), wuk_ref[...])
    for hd in range(MLA_HEADS):
        qc_ref[:, 256 * hd:256 * hd + 128] = (qlat[:, 128 * hd:128 * hd + 128] * MLA_SCALE).astype(BF16)
        qr = _rope(qc[:, 256 + 128 * hd:384 + 128 * hd], cos_q, sin_q, MLA_ROPE // 2, LANES)
        qc_ref[:, 256 * hd + 128:256 * hd + 256] = (qr * MLA_SCALE).astype(BF16)
    mb = z[:, 1408:1536]
    lane = lax.broadcasted_iota(jnp.int32, mb.shape, 1)
    kr = _rope(mb, cos_q, sin_q, MLA_ROPE // 2, LANES)
    logf = _log_sigmoid(mb + bf_ref[...])
    misc = jnp.where(lane < MLA_ROPE, kr, jnp.where(lane < LOGF_LANE + FOX_HEADS, logf, 0.0))
    misc_ref[...] = misc
    kc_ref[:, 128:256] = misc.astype(BF16)
    cos_m, sin_m = cm_ref[...], sm_ref[...]
    for blk in range(2):
        mq = _rope(z[:, 1536 + 128 * blk:1664 + 128 * blk], cos_m, sin_m, ROT_DIM // 2, HEAD_DIM)
        mq_ref[:, 128 * blk:128 * blk + 128] = (mq * FOX_SCALE).astype(BF16)
    mk = _rope(z[:, 1792:1920], cos_m, sin_m, ROT_DIM // 2, HEAD_DIM)
    mk_ref[...] = mk
    mkb_ref[...] = mk.astype(BF16)
    mv = z[:, 1920:2048]
    mv_ref[...] = mv
    mvt_ref[...] = mv.T.astype(BF16)


def _front(x, lw, tabs, tm, tab_blocks):
    m = x.shape[0]
    grid = (m // tm,)
    row = lambda w: pl.BlockSpec((tm, w), lambda i: (i, 0))
    full = lambda a: pl.BlockSpec(a.shape, lambda i: (0,) * a.ndim)
    tab = pl.BlockSpec((tm, LANES), lambda i: (i % tab_blocks, 0))
    widths = (256, 256, 256, 128, 128, 128, 128, 1024, 256, 128, 128)
    dtypes = (F32, F32, BF16, F32, F32, F32, F32, BF16, BF16, F32, F32)
    consts = (lw["g_attn"], lw["w_in"], lw["g_av"], lw["gmat"], lw["g_cq"], lw["g_ckv"], lw["w_uq"], lw["w_uk_bd"],
              lw["bf_vec"])
    tr = pl.BlockSpec((None, LANES, tm), lambda i: (i, 0, 0))
    tr_shape = jax.ShapeDtypeStruct((m // tm, LANES, tm), BF16)
    out_specs = [row(w) for w in widths] + [tr, row(256), tr, row(128), tr]
    out_shape = ([jax.ShapeDtypeStruct((m, w), d) for w, d in zip(widths, dtypes)]
                 + [tr_shape, jax.ShapeDtypeStruct((m, 256), BF16), tr_shape, jax.ShapeDtypeStruct((m, 128), BF16),
                    tr_shape])
    return pl.pallas_call(
        _front_body,
        grid=grid,
        in_specs=[row(D_MODEL)] + [full(a) for a in consts] + [tab] * 4,
        out_specs=out_specs,
        out_shape=out_shape,
        compiler_params=_cparams("parallel"),
        name="front",
    )(x, *consts, *tabs)


BIAS_K_LANE = 0
BIAS_Q_LANE = 16
N_SPLIT = 3


def _prep_body(misc_ref, fk_ref, mk_ref, k2_ref, cq_ref, kmean_ref):
    s = misc_ref.shape[0]
    r = lax.broadcasted_iota(jnp.int32, (LANES, LANES), 0)
    c = lax.broadcasted_iota(jnp.int32, (LANES, LANES), 1)
    tri = jnp.where(r >= c, 1.0, 0.0).astype(BF16)
    is_logf = (r >= LOGF_LANE) & (r < LOGF_LANE + FOX_HEADS)
    place_k = [jnp.where(is_logf & (c == BIAS_K_LANE + N_SPLIT * (r - LOGF_LANE) + t), 1.0, 0.0).astype(BF16)
               for t in range(N_SPLIT)]
    place_q = [jnp.where(is_logf & (c == BIAS_Q_LANE + N_SPLIT * (r - LOGF_LANE) + t), 1.0, 0.0).astype(BF16)
               for t in range(N_SPLIT)]
    lane = lax.broadcasted_iota(jnp.int32, (1, LANES), 1)
    n_terms = N_SPLIT * FOX_HEADS
    ones_k = jnp.where((lane >= BIAS_Q_LANE) & (lane < BIAS_Q_LANE + n_terms), 1.0, 0.0)
    ones_q = jnp.where((lane >= BIAS_K_LANE) & (lane < BIAS_K_LANE + n_terms), 1.0, 0.0)
    carry = jnp.zeros((1, LANES), F32)
    for t in range(s // LANES):
        sl = slice(t * LANES, (t + 1) * LANES)
        cs = _dot3_rhs(tri, misc_ref[sl, :]) + carry
        carry = cs[LANES - 1:LANES, :]
        parts = _split3(cs)
        ck = ones_k - (_dot(parts[0], place_k[0]) + _dot(parts[1], place_k[1]) + _dot(parts[2], place_k[2]))
        cq = ones_q + (_dot(parts[0], place_q[0]) + _dot(parts[1], place_q[1]) + _dot(parts[2], place_q[2]))
        k2_ref[sl, 0:128] = fk_ref[sl, :].astype(BF16)
        k2_ref[sl, 128:256] = ck.astype(BF16)
        cq_ref[sl, :] = cq.astype(BF16)
    nb = s // MOBA_BLOCK
    km = jnp.sum(mk_ref[...].reshape(nb, MOBA_BLOCK, LANES), axis=1) * (1.0 / MOBA_BLOCK)
    kmean_ref[...] = jnp.zeros(kmean_ref.shape, F32)
    kmean_ref[0:nb, :] = km


def _prep(misc, fk, mk, batch, seq):
    nrow = -(-(seq // MOBA_BLOCK) // 8) * 8
    return pl.pallas_call(
        _prep_body,
        grid=(batch,),
        in_specs=[pl.BlockSpec((seq, LANES), lambda b: (b, 0))] * 3,
        out_specs=[pl.BlockSpec((seq, 256), lambda b: (b, 0)),
                   pl.BlockSpec((seq, LANES), lambda b: (b, 0)),
                   pl.BlockSpec((None, nrow, LANES), lambda b: (b, 0, 0))],
        out_shape=[jax.ShapeDtypeStruct((batch * seq, 256), BF16),
                   jax.ShapeDtypeStruct((batch * seq, LANES), BF16),
                   jax.ShapeDtypeStruct((batch, nrow, LANES), F32)],
        compiler_params=_cparams("parallel"),
        name="prep",
    )(misc, fk, mk)


def _spatial_body(au_ref, avn_ref, ws_ref, bias_ref, ya_ref):
    rows = au_ref.shape[0]
    r = lax.broadcasted_iota(jnp.int32, (CHUNK, CHUNK), 0)
    c = lax.broadcasted_iota(jnp.int32, (CHUNK, CHUNK), 1)
    grp = lax.broadcasted_iota(jnp.int32, (CHUNK, BR_WIDTH), 1) // A_DIM
    ws = [jnp.where(r >= c, ws_ref[g], 0.0).astype(BF16) for g in range(4)]
    for t in range(rows // CHUNK):
        sl = slice(t * CHUNK, (t + 1) * CHUNK)
        v = avn_ref[sl, :].astype(BF16)
        s = bias_ref[...]
        for g in range(4):
            s = s + jnp.where(grp == g, _dot(ws[g], v), 0.0)
        ya_ref[sl, :] = (au_ref[sl, :] * s).astype(BF16)


def _spatial(au, avn, w_s, bias, tm):
    m = au.shape[0]
    row = pl.BlockSpec((tm, BR_WIDTH), lambda i: (i, 0))
    return pl.pallas_call(
        _spatial_body,
        grid=(m // tm,),
        in_specs=[row, row, pl.BlockSpec(w_s.shape, lambda i: (0, 0, 0)), pl.BlockSpec(bias.shape, lambda i: (0, 0))],
        out_specs=row,
        out_shape=jax.ShapeDtypeStruct((m, BR_WIDTH), BF16),
        compiler_params=_cparams("parallel"),
        name="spatial",
    )(au, avn, w_s, bias)


def _spatial1_body(au_ref, avn_ref, w0_ref, b0_ref, ya_ref):
    ya_ref[...] = (au_ref[...] * (w0_ref[...] * avn_ref[...] + b0_ref[...])).astype(BF16)


def _spatial1(au, avn, w0, b0):
    m = au.shape[0]
    return pl.pallas_call(
        _spatial1_body,
        out_shape=jax.ShapeDtypeStruct((m, BR_WIDTH), BF16),
        name="spatial1",
    )(au, avn, w0, b0)


ATT_TILE = 256


def _attn_tile(kt, vt, q_of, m_ref, l_ref, acc_ref, mask_fn):
    scores = [_dot_nt(kt, q_of(idx)) for idx in range(4)]
    probs = []
    alphas = []
    for idx in range(4):
        s = scores[idx] if mask_fn is None else mask_fn(idx, scores[idx])
        m_prev = m_ref[idx]
        m_new = jnp.maximum(m_prev, jnp.max(s, axis=0, keepdims=True))
        alpha = jnp.exp(m_prev - m_new)
        p = jnp.exp(s - m_new)
        l_ref[idx] = alpha * l_ref[idx] + jnp.sum(p, axis=0, keepdims=True)
        m_ref[idx] = m_new
        probs.append(p.astype(BF16))
        alphas.append(alpha)
    for idx in range(4):
        acc_ref[idx] = alphas[idx] * acc_ref[idx] + _dot(vt, probs[idx])


def _causal_t(t):
    return lax.broadcasted_iota(jnp.int32, (t, t), 0) <= lax.broadcasted_iota(jnp.int32, (t, t), 1)


def _past_tiles(i, k_ref, vt_ref, visit):
    def pair(jj, carry):
        j = 2 * jj
        kt = jnp.concatenate([k_ref[j], k_ref[j + 1]], axis=0)
        vt = jnp.concatenate([vt_ref[j], vt_ref[j + 1]], axis=1)
        visit(kt, vt, j, 2)
        return carry

    lax.fori_loop(0, i // 2, pair, 0)

    @pl.when(i % 2 == 1)
    def _():
        visit(k_ref[i - 1], vt_ref[i - 1], i - 1, 1)


def _init_stats(m_ref, l_ref, acc_ref):
    m_ref[...] = jnp.full(m_ref.shape, -jnp.inf, F32)
    l_ref[...] = jnp.zeros(l_ref.shape, F32)
    acc_ref[...] = jnp.zeros(acc_ref.shape, F32)


def _head_mask(qb, half):
    lane = lax.broadcasted_iota(jnp.int32, qb.shape, 1)
    keep = (lane < HEAD_DIM) if half == 0 else (lane >= HEAD_DIM)
    return jnp.where(keep, qb, jnp.zeros_like(qb))


def _merge_heads(acc_ref, l_ref, out_ref):
    row = lax.broadcasted_iota(jnp.int32, acc_ref.shape[1:], 0)
    for blk in range(2):
        o0 = acc_ref[2 * blk] / l_ref[2 * blk]
        o1 = acc_ref[2 * blk + 1] / l_ref[2 * blk + 1]
        out_ref[:, 128 * blk:128 * blk + 128] = jnp.where(row < HEAD_DIM, o0, o1).T.astype(out_ref.dtype)


def _attn_scratch(t, dk):
    return [pltpu.VMEM((4, t, dk), BF16), pltpu.VMEM((4, 1, t), F32), pltpu.VMEM((4, 1, t), F32),
            pltpu.VMEM((4, LANES, t), F32)]


def _fox_body(fq_ref, cq_ref, k2_ref, vt_ref, yb_ref, q_scr, m_ref, l_ref, acc_ref):
    i = pl.program_id(1)
    t = fq_ref.shape[0]
    _init_stats(m_ref, l_ref, acc_ref)
    lane = lax.broadcasted_iota(jnp.int32, (t, LANES), 1)
    cq = cq_ref[...]
    for blk in range(2):
        qb = fq_ref[:, 128 * blk:128 * blk + 128]
        for half in range(2):
            idx = 2 * blk + half
            lo_k = BIAS_K_LANE + N_SPLIT * HEAD_ORDER[idx]
            lo_q = BIAS_Q_LANE + N_SPLIT * HEAD_ORDER[idx]
            own = ((lane >= lo_k) & (lane < lo_k + N_SPLIT)) | ((lane >= lo_q) & (lane < lo_q + N_SPLIT))
            q_scr[idx, :, 0:128] = _head_mask(qb, half)
            q_scr[idx, :, 128:256] = jnp.where(own, cq, jnp.zeros_like(cq))
    q_of = lambda idx: q_scr[idx]
    _past_tiles(i, k2_ref, vt_ref, lambda kt, vt, j, n: _attn_tile(kt, vt, q_of, m_ref, l_ref, acc_ref, None))
    causal = _causal_t(t)
    _attn_tile(k2_ref[i], vt_ref[i], q_of, m_ref, l_ref, acc_ref, lambda idx, s: jnp.where(causal, s, NEG_INF))
    _merge_heads(acc_ref, l_ref, yb_ref)


def _seq_specs(n, t):
    qs = lambda w: pl.BlockSpec((t, w), lambda b, i: (b * n + i, 0))
    keys = lambda w: pl.BlockSpec((n, t, w), lambda b, i: (b, 0, 0))
    vals = pl.BlockSpec((n, LANES, t), lambda b, i: (b, 0, 0))
    return qs, keys, vals


def _fox_prompt(fq, cq, k2, vt, batch, seq):
    t = ATT_TILE
    n = seq // t
    qs, keys, vals = _seq_specs(n, t)
    return pl.pallas_call(
        _fox_body,
        grid=(batch, n),
        in_specs=[qs(256), qs(128), keys(256), vals],
        out_specs=qs(256),
        out_shape=jax.ShapeDtypeStruct((batch * seq, 256), BF16),
        scratch_shapes=_attn_scratch(t, 256),
        compiler_params=_cparams("parallel", "parallel"),
        name="fox_prompt",
    )(fq, cq, k2.reshape(batch * n, t, 256), vt)


def _mla_body(qc_ref, kc_ref, vt_ref, yc_ref, m_ref, l_ref, acc_ref):
    i = pl.program_id(1)
    t = qc_ref.shape[0]
    _init_stats(m_ref, l_ref, acc_ref)
    q_of = lambda hd: qc_ref[:, 256 * hd:256 * hd + 256]
    _past_tiles(i, kc_ref, vt_ref, lambda kt, vt, j, n: _attn_tile(kt, vt, q_of, m_ref, l_ref, acc_ref, None))
    causal = _causal_t(t)
    _attn_tile(kc_ref[i], vt_ref[i], q_of, m_ref, l_ref, acc_ref, lambda idx, s: jnp.where(causal, s, NEG_INF))
    for hd in range(MLA_HEADS):
        yc_ref[:, 128 * hd:128 * hd + 128] = (acc_ref[hd] / l_ref[hd]).T.astype(BF16)


def _mla_prompt(qc, kc, vt, batch, seq):
    t = ATT_TILE
    n = seq // t
    qs, keys, vals = _seq_specs(n, t)
    return pl.pallas_call(
        _mla_body,
        grid=(batch, n),
        in_specs=[qs(1024), keys(256), vals],
        out_specs=qs(512),
        out_shape=jax.ShapeDtypeStruct((batch * seq, 512), BF16),
        scratch_shapes=_attn_scratch(t, 256)[1:],
        compiler_params=_cparams("parallel", "parallel"),
        name="mla_prompt",
    )(qc, kc.reshape(batch * n, t, 256), vt)


def _topk_rank(g, n_cand):
    lane = lax.broadcasted_iota(jnp.int32, g.shape, 1)
    rank = jnp.zeros(g.shape, F32)
    for mth in range(n_cand):
        gm = g[:, mth:mth + 1]
        beats = (gm > g) | ((gm == g) & (mth < lane))
        rank = rank + jnp.where(beats, 1.0, 0.0)
    return rank


def _moba_body(mq_ref, k_ref, vt_ref, kmean_ref, yd_ref, q_scr, m_ref, l_ref, acc_ref, sel_ref):
    i = pl.program_id(1)
    t = mq_ref.shape[0]
    nrow = kmean_ref.shape[0]
    _init_stats(m_ref, l_ref, acc_ref)
    km = _split3(kmean_ref[...])
    blk_id = lax.broadcasted_iota(jnp.int32, (nrow, t), 0)
    for blk in range(2):
        qb = mq_ref[:, 128 * blk:128 * blk + 128]
        for half in range(2):
            idx = 2 * blk + half
            qm = _head_mask(qb, half)
            q_scr[idx] = qm
            gate = (_dot_nt(km[0], qm) + _dot_nt(km[1], qm) + _dot_nt(km[2], qm)) * (1.0 / FOX_SCALE)
            g = jnp.where(blk_id < i, gate, NEG_INF)
            rank = jnp.zeros((nrow, t), F32)
            for mth in range(nrow):
                gm = g[mth:mth + 1, :]
                beats = (gm > g) | ((gm == g) & (mth < blk_id))
                rank = rank + jnp.where(beats, 1.0, 0.0)
            sel_ref[idx] = jnp.where((rank < MOBA_TOPK) & (g > 0.5 * NEG_INF), 1.0, 0.0)
    q_of = lambda idx: q_scr[idx]
    causal = _causal_t(t)
    _attn_tile(k_ref[i], vt_ref[i], q_of, m_ref, l_ref, acc_ref, lambda idx, s: jnp.where(causal, s, NEG_INF))

    def visit(kt, vt, j, n):
        def picked(idx, s):
            sel = sel_ref[idx, pl.ds(j, 1), :]
            if n == 2:
                first = lax.broadcasted_iota(jnp.int32, s.shape, 0) < t
                sel = jnp.where(first, sel, sel_ref[idx, pl.ds(j + 1, 1), :])
            return jnp.where(sel > 0.0, s, NEG_INF)
        _attn_tile(kt, vt, q_of, m_ref, l_ref, acc_ref, picked)

    _past_tiles(i, k_ref, vt_ref, visit)
    _merge_heads(acc_ref, l_ref, yd_ref)


def _moba_prompt(mq, mkb, vt, kmean, batch, seq):
    t = MOBA_BLOCK
    n = seq // t
    nrow = kmean.shape[1]
    qs, keys, vals = _seq_specs(n, t)
    return pl.pallas_call(
        _moba_body,
        grid=(batch, n),
        in_specs=[qs(256), keys(128), vals, pl.BlockSpec((None, nrow, LANES), lambda b, i: (b, 0, 0))],
        out_specs=qs(256),
        out_shape=jax.ShapeDtypeStruct((batch * seq, 256), BF16),
        scratch_shapes=_attn_scratch(t, 128) + [pltpu.VMEM((4, nrow, t), F32)],
        compiler_params=_cparams("parallel", "parallel"),
        name="moba_prompt",
    )(mq, mkb.reshape(batch * n, t, 128), vt, kmean)


def _mix_body(x_ref, ga_ref, ya_ref, yb_ref, ycl_ref, yd_ref, wg_ref, wbr_ref, wuv_ref, wo_ref, x1_ref):
    x = x_ref[...]
    h = _rms(x, ga_ref[...]).astype(BF16)
    yc = _dot(ycl_ref[...], wuv_ref[...]).astype(BF16)
    ys = (ya_ref[...], yb_ref[...], yc, yd_ref[...])
    merged = None
    for b in range(4):
        term = jax.nn.sigmoid(_dot(h, wg_ref[b])) * _dot(ys[b], wbr_ref[b])
        merged = term if merged is None else merged + term
    x1_ref[...] = x + _dot(merged.astype(BF16), wo_ref[...])


def _mix(x, ya, yb, ycl, yd, lw, tm):
    m = x.shape[0]
    row = lambda w: pl.BlockSpec((tm, w), lambda i: (i, 0))
    full = lambda a: pl.BlockSpec(a.shape, lambda i: (0,) * a.ndim)
    consts = (lw["w_gate"], lw["w_br"], lw["w_uv_bd"], lw["w_o"])
    return pl.pallas_call(
        _mix_body,
        grid=(m // tm,),
        in_specs=[row(D_MODEL), full(lw["g_attn"]), row(256), row(256), row(512), row(256)] + [full(a) for a in consts],
        out_specs=row(D_MODEL),
        out_shape=jax.ShapeDtypeStruct((m, D_MODEL), F32),
        compiler_params=_cparams("parallel"),
        name="mix",
    )(x, lw["g_attn"], ya, yb, ycl, yd, *consts)


def _mlp_body(x_ref, g_ref, wup_ref, wdn_ref, gf_ref, o_ref, *, final):
    x = x_ref[...]
    h = _rms(x, g_ref[...]).astype(BF16)
    acc = x
    fc = 1024
    for c in range(D_FF // fc):
        u = jnp.maximum(_dot(h, wup_ref[:, c * fc:(c + 1) * fc]), 0.0)
        acc = acc + _dot((u * u).astype(BF16), wdn_ref[c * fc:(c + 1) * fc, :])
    o_ref[...] = _rms(acc, gf_ref[...]) if final else acc


def _mlp(x, lw, g_final, final, tm):
    m = x.shape[0]
    row = pl.BlockSpec((tm, D_MODEL), lambda i: (i, 0))
    full = lambda a: pl.BlockSpec(a.shape, lambda i: (0,) * a.ndim)
    consts = (lw["g_mlp"], lw["w_up"], lw["w_down"], g_final)
    return pl.pallas_call(
        functools.partial(_mlp_body, final=final),
        grid=(m // tm,),
        in_specs=[row] + [full(a) for a in consts],
        out_specs=row,
        out_shape=jax.ShapeDtypeStruct((m, D_MODEL), F32),
        compiler_params=_cparams("parallel"),
        name="mlp",
    )(x, *consts)


def _perm_heads(w, axis):
    parts = [lax.slice_in_dim(w, h * HEAD_DIM, (h + 1) * HEAD_DIM, axis=axis) for h in HEAD_ORDER]
    return jnp.concatenate(parts, axis=axis)


def _layer_weights(l, g_attn, w_in, w_gate, g_av, w_s, b_s, fox_bf, g_cq, g_ckv, w_uq, w_uk, w_uv, w_br, w_o,
                   g_mlp, w_up, w_down):
    wi = w_in[l]
    a_u, a_v, fq, fk, fv = wi[:, 0:256], wi[:, 256:512], wi[:, 512:768], wi[:, 768:896], wi[:, 896:1024]
    ff, cq, ckv, kr = wi[:, 1024:1028], wi[:, 1028:1284], wi[:, 1284:1412], wi[:, 1412:1444]
    mq, mk, mv = wi[:, 1444:1700], wi[:, 1700:1828], wi[:, 1828:1956]
    misc = jnp.concatenate([kr, ff, jnp.zeros((D_MODEL, LANES - MLA_ROPE - FOX_HEADS), F32)], axis=1)
    w_in_p = jnp.concatenate([a_u, a_v, _perm_heads(fq, 1), fk, fv, cq, ckv, misc, _perm_heads(mq, 1), mk, mv], axis=1)
    uq = w_uq[l]
    zpad = jnp.zeros((MLA_Q_LORA, LANES - MLA_ROPE), F32)
    uq_p = jnp.concatenate([uq[:, :, :MLA_NOPE].reshape(MLA_Q_LORA, MLA_HEADS * MLA_NOPE)]
                           + [a for h in range(MLA_HEADS) for a in (uq[:, h, MLA_NOPE:], zpad)], axis=1)
    uk_bd = jnp.zeros((MLA_HEADS * MLA_NOPE, MLA_HEADS * MLA_KV_LORA), F32)
    uv_bd = jnp.zeros((MLA_HEADS * MLA_KV_LORA, BR_WIDTH), F32)
    for h in range(MLA_HEADS):
        uk_bd = uk_bd.at[64 * h:64 * h + 64, 128 * h:128 * h + 128].set(w_uk[l][:, h, :].T)
        uv_bd = uv_bd.at[128 * h:128 * h + 128, 64 * h:64 * h + 64].set(w_uv[l][:, h, :])
    br = jnp.stack([w_br[l, 0], _perm_heads(w_br[l, 1], 0), w_br[l, 2], _perm_heads(w_br[l, 3], 0)], axis=0)
    grp = np.arange(BR_WIDTH) // A_DIM
    return {
        "g_attn": g_attn[l][None, :], "w_in": w_in_p.astype(BF16),
        "g_av": g_av[l].reshape(1, BR_WIDTH), "gmat": jnp.asarray(grp[:, None] == grp[None, :], BF16),
        "g_cq": g_cq[l][None, :], "g_ckv": g_ckv[l][None, :],
        "w_uq": uq_p.astype(BF16), "w_uk_bd": uk_bd.astype(BF16), "w_uv_bd": uv_bd.astype(BF16),
        "bf_vec": jnp.zeros((1, LANES), F32).at[0, LOGF_LANE:LOGF_LANE + FOX_HEADS].set(fox_bf[l]),
        "w_s": w_s[l], "bias": jnp.repeat(b_s[l].T, A_DIM, axis=1),
        "w0": jnp.repeat(w_s[l][:, 0, 0], A_DIM)[None, :], "b0": jnp.repeat(b_s[l][:, 0], A_DIM)[None, :],
        "w_gate": w_gate[l].astype(BF16), "w_br": br.astype(BF16), "w_o": w_o[l].astype(BF16),
        "g_mlp": g_mlp[l][None, :], "w_up": w_up[l].astype(BF16), "w_down": w_down[l].astype(BF16),
    }


def _rope_tables(pos):
    posf = pos.astype(F32)[:, None]
    lane = np.arange(LANES)

    def tables(half, period):
        inv = ROPE_THETA ** (-jnp.arange(half, dtype=F32) / half)
        ang = posf * inv[None, :]
        j = lane % period
        idx = j % half
        rot = j < 2 * half
        cos = jnp.where(rot[None, :], jnp.cos(ang)[:, idx], 1.0)
        sgn = np.where(j < half, -1.0, 1.0).astype(np.float32)
        sin = jnp.where(rot[None, :], jnp.sin(ang)[:, idx] * sgn[None, :], 0.0)
        return cos, sin

    cq, sq = tables(MLA_ROPE // 2, LANES)
    cm, sm = tables(ROT_DIM // 2, HEAD_DIM)
    return cq, sq, cm, sm


def _prompt_layer(x, lw, tabs, batch, seq, g_final, final):
    (au, avn, fq, fk, fv, misc, ckvn, qc, mq, mk, mv,
     fvt, kc, ckvt, mkb, mvt) = _front(x, lw, tabs, ATT_TILE, seq // ATT_TILE)
    k2, cq, kmean = _prep(misc, fk, mk, batch, seq)
    ya = _spatial(au, avn, lw["w_s"], lw["bias"], 512)
    yb = _fox_prompt(fq, cq, k2, fvt, batch, seq)
    ycl = _mla_prompt(qc, kc, ckvt, batch, seq)
    yd = _moba_prompt(mq, mkb, mvt, kmean, batch, seq)
    x1 = _mix(x, ya, yb, ycl, yd, lw, 512)
    x2 = _mlp(x1, lw, g_final, final, 512)
    return x2, (fk, fv, misc, ckvn, mk, mv, avn)


def _unpack_rows(rows, lead):
    fk, fv, misc, ckvn, mk, mv, avn = rows
    return (fk.reshape(lead + (2, HEAD_DIM)), fv.reshape(lead + (2, HEAD_DIM)),
            misc[:, LOGF_LANE:LOGF_LANE + FOX_HEADS].reshape(lead + (FOX_HEADS,)),
            ckvn.reshape(lead + (MLA_KV_LORA,)), misc[:, :MLA_ROPE].reshape(lead + (MLA_ROPE,)),
            mk.reshape(lead + (2, HEAD_DIM)), mv.reshape(lead + (2, HEAD_DIM)),
            avn.reshape(lead + (4, A_DIM)))


def _prompt_trunk(x_prompt, lws, g_final):
    batch, seq, _ = x_prompt.shape
    tabs = _rope_tables(jnp.arange(seq))
    x = x_prompt.reshape(batch * seq, D_MODEL)
    rows = []
    for l, lw in enumerate(lws):
        x, r = _prompt_layer(x, lw, tabs, batch, seq, g_final[None, :], l == len(lws) - 1)
        rows.append(_unpack_rows(r, (batch, seq)))
    return x.reshape(batch, seq, D_MODEL), tuple(jnp.stack(r, axis=0) for r in zip(*rows))


SWEEP_CHUNK = 64


def _scan_body(x_ref, w8_ref, tb8_ref):
    r = lax.broadcasted_iota(jnp.int32, (LANES, LANES), 0)
    c = lax.broadcasted_iota(jnp.int32, (LANES, LANES), 1)
    later = jnp.where(r > c, 1.0, 0.0).astype(BF16)
    ones = jnp.ones((LANES, LANES), BF16)
    eo = lax.broadcasted_iota(jnp.int32, (2 * LANES, LANES), 0)
    ei = lax.broadcasted_iota(jnp.int32, (2 * LANES, LANES), 1)
    expand = jnp.where(((eo >> 3) == (ei >> 2)) & ((eo & 7) == (ei & 3)), 1.0, 0.0).astype(BF16)
    for t in range(x_ref.shape[0] // LANES):
        x = x_ref[t * LANES:(t + 1) * LANES, :]
        w8_ref[2 * t * LANES:2 * (t + 1) * LANES, :] = _dot3_rhs(expand, _dot3_lhs(x, later))
        tb8_ref[2 * t * LANES:2 * (t + 1) * LANES, :] = _dot3_rhs(expand, _dot3_lhs(x, ones))


def _logf_scan(lf2):
    rows = lf2.shape[0]
    rb = 1024
    return pl.pallas_call(
        _scan_body,
        grid=(rows // rb,),
        in_specs=[pl.BlockSpec((rb, LANES), lambda i: (i, 0))],
        out_specs=[pl.BlockSpec((2 * rb, LANES), lambda i: (i, 0))] * 2,
        out_shape=[jax.ShapeDtypeStruct((2 * rows, LANES), F32)] * 2,
        compiler_params=_cparams("parallel"),
        name="logf_scan",
    )(lf2)


def _q8_pair(q_ref):
    qa, qb = q_ref[:, 0:128].astype(F32), q_ref[:, 128:256].astype(F32)
    row = lax.broadcasted_iota(jnp.int32, (8, LANES), 0)
    lo = lax.broadcasted_iota(jnp.int32, (8, LANES), 1) < HEAD_DIM
    zero = jnp.zeros((8, LANES), F32)
    q8 = jnp.where((row == 0) & lo, qa, jnp.where((row == 1) & lo, qb,
                   jnp.where((row == 2) & ~lo, qb, jnp.where((row == 3) & ~lo, qa, zero))))
    return q8.astype(BF16)


def _pair_out(o8, out_ref):
    lo = lax.broadcasted_iota(jnp.int32, (1, LANES), 1) < HEAD_DIM
    out_ref[:, 0:128] = jnp.where(lo, o8[0:1], o8[3:4]).astype(out_ref.dtype)
    out_ref[:, 128:256] = jnp.where(lo, o8[1:2], o8[2:3]).astype(out_ref.dtype)


def _chunk_probs(s, m_ref, l_ref):
    mx = s[0]
    for si in s[1:]:
        mx = jnp.maximum(mx, si)
    m_prev = m_ref[...]
    m_new = jnp.maximum(m_prev, jnp.max(mx, axis=-1, keepdims=True))
    alpha = jnp.exp(m_prev - m_new)
    psum = None
    probs = []
    for si in s:
        p = jnp.exp(si - m_new)
        psum = p if psum is None else psum + p
        probs.append(p.astype(BF16))
    l_ref[...] = alpha * l_ref[...] + jnp.sum(psum, axis=-1, keepdims=True)
    m_ref[...] = m_new
    return probs, alpha


def _chunk_values(probs, alpha, vals, acc_ref, pv_fn):
    pv = None
    for p, vi in zip(probs, vals):
        t = pv_fn(p, vi)
        pv = t if pv is None else pv + t
    acc_ref[...] = alpha * acc_ref[...] + pv


def _chunk_softmax(s, vals, m_ref, l_ref, acc_ref, pv_fn):
    probs, alpha = _chunk_probs(s, m_ref, l_ref)
    _chunk_values(probs, alpha, vals, acc_ref, pv_fn)


def _paged_schedule(copies):
    b, j = pl.program_id(0), pl.program_id(1)
    nch = pl.num_programs(1)
    lin = b * nch + j
    slot = lin % 2

    @pl.when(lin == 0)
    def _():
        for d in copies(b, j, slot):
            d.start()

    @pl.when(lin + 1 < pl.num_programs(0) * nch)
    def _():
        nxt = lin + 1
        for d in copies(nxt // nch, nxt % nch, 1 - slot):
            d.start()

    for d in copies(b, j, slot):
        d.wait()
    return slot


def _tok(w):
    return pl.BlockSpec((None, 1, w), lambda b, j, *_: (b, 0, 0))


_ANY = pl.BlockSpec(memory_space=pl.ANY)


def _dot2_rhs(a_bf16, b_f32):
    b1 = b_f32.astype(BF16)
    b2 = (b_f32 - b1.astype(F32)).astype(BF16)
    return _dot(a_bf16, b1) + _dot(a_bf16, b2)


def _sweep_body(pt_ref, fq_ref, fk_ref, fv_ref, misc_ref, qc_ref, ckvn_ref, mq_ref,
                ktf_hbm, vtf_hbm, w8_hbm, tb8_hbm, ckv_hbm, krt_hbm, ktm_hbm,
                yb_ref, yc_ref, idx_ref,
                kbuf, vbuf, wbuf, tbuf, cbuf, rbuf, gbuf, sem,
                fm_ref, fl_ref, facc_ref, off_ref, cm_ref, cl_ref, cacc_ref, g_ref, *, layer, ch):
    j = pl.program_id(1)
    nch = pl.num_programs(1)
    chunk = nch - 1 - j
    pairs = ((ktf_hbm, kbuf), (vtf_hbm, vbuf), (w8_hbm, wbuf), (tb8_hbm, tbuf), (ckv_hbm, cbuf), (krt_hbm, rbuf),
             (ktm_hbm, gbuf))

    def copies(bb, jj, sl):
        base = (nch - 1 - jj) * ch
        out = []
        for i in range(ch):
            page = pt_ref[bb, base + i]
            for hbm, buf in pairs:
                out.append(pltpu.make_async_copy(hbm.at[layer, page], buf.at[sl, i], sem.at[sl]))
        return out

    slot = _paged_schedule(copies)
    row = lax.broadcasted_iota(jnp.int32, (8, LANES), 0)
    lane = lax.broadcasted_iota(jnp.int32, (8, LANES), 1)
    q8f = _q8_pair(fq_ref)
    q8m = _q8_pair(mq_ref)
    zero = jnp.zeros((8, LANES), F32)
    qlf, qrf = zero, zero
    for hd in range(MLA_HEADS):
        qlf = jnp.where(row == hd, qc_ref[:, 256 * hd:256 * hd + 128].astype(F32), qlf)
        qrf = jnp.where(row == hd, qc_ref[:, 256 * hd + 128:256 * hd + 256].astype(F32), qrf)
    ql, qr32 = qlf.astype(BF16), qrf.astype(BF16)[:, 0:MLA_ROPE]

    @pl.when(j == 0)
    def _():
        fm_ref[...] = jnp.sum(q8f.astype(F32) * fk_ref[...], axis=-1, keepdims=True)
        fl_ref[...] = jnp.ones(fl_ref.shape, F32)
        facc_ref[...] = jnp.broadcast_to(fv_ref[...], facc_ref.shape)
        off = zero
        for hd in range(FOX_HEADS):
            off = jnp.where(row == hd, misc_ref[:, LOGF_LANE + hd:LOGF_LANE + hd + 1], off)
        off_ref[...] = off
        cm_ref[...] = jnp.sum(qlf * ckvn_ref[...] + qrf * misc_ref[...], axis=-1, keepdims=True)
        cl_ref[...] = jnp.ones(cl_ref.shape, F32)
        cacc_ref[...] = jnp.broadcast_to(ckvn_ref[...], cacc_ref.shape)
        g_ref[...] = zero

    run = off_ref[...]
    after = [None] * ch
    for i in reversed(range(ch)):
        after[i] = run
        run = run + tbuf[slot, i]
    off_ref[...] = run
    s_f = [_dot(q8f, kbuf[slot, i].astype(BF16)) + (wbuf[slot, i] + after[i]) for i in range(ch)]
    lat = [cbuf[slot, i].astype(BF16) for i in range(ch)]
    s_c = [_dot_nt(ql, lat[i]) + _dot(qr32, rbuf[slot, i].astype(BF16)) for i in range(ch)]
    ppb = MOBA_BLOCK // PAGE_SIZE
    nblk = ch // ppb
    g = g_ref[...]
    for n in range(nblk):
        raw = None
        for u in range(ppb):
            t = _dot2_rhs(q8m, gbuf[slot, n * ppb + u])
            raw = t if raw is None else raw + t
        g = jnp.where(lane == chunk * nblk + n, jnp.sum(raw, axis=-1, keepdims=True), g)
    g_ref[...] = g
    p_f, alpha_f = _chunk_probs(s_f, fm_ref, fl_ref)
    p_c, alpha_c = _chunk_probs(s_c, cm_ref, cl_ref)
    _chunk_values(p_f, alpha_f, [vbuf[slot, i] for i in range(ch)], facc_ref, lambda p, v: _dot_nt(p, v.astype(BF16)))
    _chunk_values(p_c, alpha_c, lat, cacc_ref, _dot)

    @pl.when(j == nch - 1)
    def _():
        _pair_out(facc_ref[...] / fl_ref[...], yb_ref)
        o8 = cacc_ref[...] / cl_ref[...]
        for hd in range(MLA_HEADS):
            yc_ref[:, 128 * hd:128 * hd + 128] = o8[hd:hd + 1].astype(BF16)
        ncand = nch * nblk
        gate = jnp.where(lane < ncand, g * (1.0 / (FOX_SCALE * MOBA_BLOCK)), NEG_INF)
        rank = _topk_rank(gate, ncand)
        lane_f = lane.astype(F32)
        out = zero
        for t in range(MOBA_TOPK):
            pick = jnp.sum(jnp.where(rank == t, lane_f, 0.0), axis=-1, keepdims=True)
            out = jnp.where(lane == t, pick, out)
        idx_ref[...] = out.astype(jnp.int32)


def _sweep_decode(pt, fq, fk, fv, misc, qc, ckvn, mq, views, layer):
    kt_f, vt_f, w8, tb8, ckv, krt, kt_m = views
    nseq, npages = pt.shape
    ch = SWEEP_CHUNK
    page = lambda rows: pltpu.VMEM((2, ch, rows, LANES), F32)
    gs = pltpu.PrefetchScalarGridSpec(
        num_scalar_prefetch=1,
        grid=(nseq, npages // ch),
        in_specs=[_tok(256), _tok(128), _tok(128), _tok(128), _tok(1024), _tok(128), _tok(256)] + [_ANY] * 7,
        out_specs=[_tok(256), _tok(512), pl.BlockSpec((None, 8, LANES), lambda b, j, *_: (b, 0, 0))],
        scratch_shapes=[page(LANES), page(LANES), page(8), page(8), page(PAGE_SIZE), page(MLA_ROPE), page(LANES),
                        pltpu.SemaphoreType.DMA((2,)),
                        pltpu.VMEM((8, 1), F32), pltpu.VMEM((8, 1), F32), pltpu.VMEM((8, LANES), F32),
                        pltpu.VMEM((8, LANES), F32),
                        pltpu.VMEM((8, 1), F32), pltpu.VMEM((8, 1), F32), pltpu.VMEM((8, LANES), F32),
                        pltpu.VMEM((8, LANES), F32)],
    )
    return pl.pallas_call(
        functools.partial(_sweep_body, layer=layer, ch=ch),
        grid_spec=gs,
        out_shape=[jax.ShapeDtypeStruct((nseq, 1, 256), BF16), jax.ShapeDtypeStruct((nseq, 1, 512), BF16),
                   jax.ShapeDtypeStruct((nseq, 8, LANES), jnp.int32)],
        compiler_params=_cparams("arbitrary", "arbitrary"),
        name="sweep_decode",
    )(pt, fq, fk, fv, misc, qc, ckvn, mq, kt_f, vt_f, w8, tb8, ckv, krt, kt_m)


def _moba_dec_body(pt_ref, idx_ref, mq_ref, mk_ref, mv_ref, kt_hbm, vt_hbm, yd_ref,
                   kbuf, vbuf, sem, m_ref, l_ref, acc_ref, *, layer):
    ppb = MOBA_BLOCK // PAGE_SIZE
    heads = 4
    nsel = heads * MOBA_TOPK * ppb

    def copies(bb, jj, sl):
        out = []
        for hd in range(heads):
            for t in range(MOBA_TOPK):
                blk = idx_ref[(bb * heads + hd) * MOBA_TOPK + t]
                for u in range(ppb):
                    page = pt_ref[bb, blk * ppb + u]
                    at = (hd * MOBA_TOPK + t) * ppb + u
                    out.append(pltpu.make_async_copy(kt_hbm.at[layer, page], kbuf.at[sl, at], sem.at[sl]))
                    out.append(pltpu.make_async_copy(vt_hbm.at[layer, page], vbuf.at[sl, at], sem.at[sl]))
        return out

    slot = _paged_schedule(copies)
    q8 = _q8_pair(mq_ref)
    row = lax.broadcasted_iota(jnp.int32, (8, LANES), 0)
    m_ref[...] = jnp.sum(q8.astype(F32) * mk_ref[...], axis=-1, keepdims=True)
    l_ref[...] = jnp.ones(l_ref.shape, F32)
    acc_ref[...] = jnp.broadcast_to(mv_ref[...], acc_ref.shape)
    s = [jnp.where(row == at // (MOBA_TOPK * ppb), _dot(q8, kbuf[slot, at].astype(BF16)), NEG_INF)
         for at in range(nsel)]
    vals = [vbuf[slot, at] for at in range(nsel)]
    _chunk_softmax(s, vals, m_ref, l_ref, acc_ref, lambda p, v: _dot_nt(p, v.astype(BF16)))
    _pair_out(acc_ref[...] / l_ref[...], yd_ref)


def _moba_decode(pt, idx, mq, mk, mv, kt, vt, layer):
    nseq = pt.shape[0]
    nsel = 4 * MOBA_TOPK * (MOBA_BLOCK // PAGE_SIZE)
    gs = pltpu.PrefetchScalarGridSpec(
        num_scalar_prefetch=2,
        grid=(nseq, 1),
        in_specs=[_tok(256), _tok(128), _tok(128), _ANY, _ANY],
        out_specs=_tok(256),
        scratch_shapes=[pltpu.VMEM((2, nsel, LANES, LANES), F32), pltpu.VMEM((2, nsel, LANES, LANES), F32),
                        pltpu.SemaphoreType.DMA((2,)),
                        pltpu.VMEM((8, 1), F32), pltpu.VMEM((8, 1), F32), pltpu.VMEM((8, LANES), F32)],
    )
    return pl.pallas_call(
        functools.partial(_moba_dec_body, layer=layer),
        grid_spec=gs,
        out_shape=jax.ShapeDtypeStruct((nseq, 1, 256), BF16),
        compiler_params=_cparams("arbitrary", "arbitrary"),
        name="moba_decode",
    )(pt, idx, mq, mk, mv, kt, vt)


def _sample_layer(x, lw, tabs, pt, views, layer, g_final, final):
    kt_f, vt_f, w8, tb8, ckv, krt, kt_m, vt_m = views
    n = x.shape[0]
    au, avn, fq, fk, fv, misc, ckvn, qc, mq, mk, mv = _front(x, lw, tabs, n, 1)[:11]
    r3 = lambda a: a.reshape(n, 1, a.shape[-1])
    ya = _spatial1(au, avn, lw["w0"], lw["b0"])
    yb, ycl, idx = _sweep_decode(pt, r3(fq), r3(fk), r3(fv), r3(misc), r3(qc), r3(ckvn), r3(mq), views[:7], layer)
    idx = idx[:, 0:4, 0:MOBA_TOPK].reshape(-1)
    yd = _moba_decode(pt, idx, r3(mq), r3(mk), r3(mv), kt_m, vt_m, layer)
    x1 = _mix(x, ya, yb.reshape(n, 256), ycl.reshape(n, 512), yd.reshape(n, 256), lw, n)
    x2 = _mlp(x1, lw, g_final, final, n)
    return x2, (fk, fv, misc, ckvn, mk, mv, avn)


def _page_views(cache_fox_k, cache_fox_v, cache_fox_logf, cache_mla_ckv, cache_mla_krope, cache_moba_k, cache_moba_v):
    depth, pool = cache_fox_k.shape[:2]
    kt = lambda c: jnp.transpose(c, (0, 1, 3, 4, 2)).reshape(depth, pool, 2 * HEAD_DIM, PAGE_SIZE)
    lf2 = jnp.transpose(cache_fox_logf, (0, 1, 3, 2)).reshape(depth * pool * FOX_HEADS, PAGE_SIZE)
    w8, tb8 = _logf_scan(lf2)
    w8 = w8.reshape(depth, pool, 8, PAGE_SIZE)
    tb8 = tb8.reshape(depth, pool, 8, PAGE_SIZE)
    krt = jnp.transpose(cache_mla_krope, (0, 1, 3, 2))
    return kt(cache_fox_k), kt(cache_fox_v), w8, tb8, cache_mla_ckv, krt, kt(cache_moba_k), kt(cache_moba_v)


def _sample_trunk(x_sample, lws, g_final, page_table, views):
    n = x_sample.shape[0]
    past_len = page_table.shape[1] * PAGE_SIZE
    tabs = _rope_tables(jnp.full((n,), past_len, jnp.int32))
    x = x_sample.reshape(n, D_MODEL)
    rows = []
    for l, lw in enumerate(lws):
        x, r = _sample_layer(x, lw, tabs, page_table, views, l, g_final[None, :], l == len(lws) - 1)
        rows.append(_unpack_rows(r, (n, 1)))
    return x.reshape(n, 1, D_MODEL), tuple(jnp.stack(r, axis=0) for r in zip(*rows))


def kernel(x_prompt, x_sample, cache_fox_k, cache_fox_v, cache_fox_logf, cache_mla_ckv, cache_mla_krope,
           cache_moba_k, cache_moba_v, page_table, g_attn, w_in, w_gate, g_av, w_s, b_s, fox_bf, g_cq, g_ckv,
           w_uq, w_uk, w_uv, w_br, w_o, g_mlp, w_up, w_down, g_final):
    assert x_sample.shape[1] == 1, "sample group is one new token per sequence"
    depth = w_in.shape[0]
    lws = [_layer_weights(l, g_attn, w_in, w_gate, g_av, w_s, b_s, fox_bf, g_cq, g_ckv, w_uq, w_uk, w_uv, w_br, w_o,
                          g_mlp, w_up, w_down) for l in range(depth)]
    y_prompt, p_rows = _prompt_trunk(x_prompt, lws, g_final)
    views = _page_views(cache_fox_k, cache_fox_v, cache_fox_logf, cache_mla_ckv, cache_mla_krope,
                        cache_moba_k, cache_moba_v)
    y_sample, s_rows = _sample_trunk(x_sample, lws, g_final, page_table, views)
    return (y_prompt, y_sample) + tuple(p_rows[:7]) + tuple(s_rows)
```
